```python
import math
import jax, jax.numpy as jnp
from jax import lax
import numpy as np

D_MODEL = 1024
BATCH = 8
SEQ = 4096
DEPTH = 2

CHUNK = 64
N_LEFT_CHUNKS = 8
BAND = (N_LEFT_CHUNKS + 1) * CHUNK
Q_BLOCK = 128
HEAD_DIM = 64
WIDTH_A = D_MODEL // 2
WIDTH_B = D_MODEL - WIDTH_A
N_HEADS_A = WIDTH_A // HEAD_DIM
N_HEADS_B = WIDTH_B // (2 * HEAD_DIM)
REL_CLIP = 128
ROPE_THETA = 10000.0
DEEPNORM_ALPHA = (2 * DEPTH) ** 0.25
DEEPNORM_BETA = (8 * DEPTH) ** -0.25
LN_EPS = 1e-5
RMS_EPS = 1e-5
NEG_INF = -1e30
PROJ_COLS = 4 * WIDTH_A + 4 * WIDTH_B

kernel_name = "hybrid_chunked_relpos_diff_attention_deepnorm"


def layer_norm(x, g, b):
    xf = x.astype(jnp.float32)
    mu = jnp.mean(xf, axis=-1, keepdims=True)
    var = jnp.mean(jnp.square(xf - mu), axis=-1, keepdims=True)
    y = (xf - mu) * lax.rsqrt(var + LN_EPS) * g.astype(jnp.float32) + b.astype(jnp.float32)
    return y.astype(x.dtype)


def rms_norm(x, g):
    xf = x.astype(jnp.float32)
    y = xf * lax.rsqrt(jnp.mean(jnp.square(xf), axis=-1, keepdims=True) + RMS_EPS) * g.astype(jnp.float32)
    return y.astype(x.dtype)


def rope(x, pos):
    d = x.shape[-1]
    inv_freq = ROPE_THETA ** (-jnp.arange(0, d, 2, dtype=jnp.float32) / d)
    ang = pos[:, None] * inv_freq[None, :]
    ang = jnp.concatenate([ang, ang], axis=-1)[None, :, None, None, :]
    xf = x.astype(jnp.float32)
    x1, x2 = jnp.split(xf, 2, axis=-1)
    rot = jnp.concatenate([-x2, x1], axis=-1)
    return (xf * jnp.cos(ang) + rot * jnp.sin(ang)).astype(x.dtype)


def chunked_relpos_attention(q, k, v, bias_table):
    B, S, H, D = q.shape
    n_chunks = S // CHUNK
    pad = BAND - CHUNK
    k_pad = jnp.pad(k, ((0, 0), (pad, 0), (0, 0), (0, 0)))
    v_pad = jnp.pad(v, ((0, 0), (pad, 0), (0, 0), (0, 0)))
    rel = pad + jnp.arange(CHUNK)[:, None] - jnp.arange(BAND)[None, :]
    bias = bias_table[:, jnp.clip(rel, -REL_CLIP, REL_CLIP) + REL_CLIP].astype(jnp.float32)
    q_chunks = q.reshape(B, n_chunks, CHUNK, H, D).transpose(1, 0, 2, 3, 4)
    scale = D ** -0.5

    def one_chunk(args):
        qc, c = args
        start = c * CHUNK
        kc = lax.dynamic_slice_in_dim(k_pad, start, BAND, axis=1)
        vc = lax.dynamic_slice_in_dim(v_pad, start, BAND, axis=1)
        s = jnp.einsum('bqhd,bkhd->bhqk', qc, kc).astype(jnp.float32) * scale + bias[None]
        valid = (start - pad + jnp.arange(BAND)) >= 0
        s = jnp.where(valid[None, None, None, :], s, NEG_INF)
        p = jax.nn.softmax(s, axis=-1).astype(vc.dtype)
        return jnp.einsum('bhqk,bkhd->bqhd', p, vc)

    out = lax.map(one_chunk, (q_chunks, jnp.arange(n_chunks)))
    return out.transpose(1, 0, 2, 3, 4).reshape(B, S, H * D)


def differential_attention(q, k, v, lam, subln_g, lam_init):
    B, S, H, _, D = q.shape
    n_blocks = S // Q_BLOCK
    q_blocks = q.reshape(B, n_blocks, Q_BLOCK, H, 2, D).transpose(1, 0, 2, 3, 4, 5)
    key_chunk = jnp.arange(S) // CHUNK
    scale = D ** -0.5

    def one_block(args):
        qb, i = args
        q_chunk = (i * Q_BLOCK + jnp.arange(Q_BLOCK)) // CHUNK
        s = jnp.einsum('bqhnd,bkhnd->bhnqk', qb, k).astype(jnp.float32) * scale
        mask = key_chunk[None, :] <= q_chunk[:, None]
        s = jnp.where(mask[None, None, None], s, NEG_INF)
        p = jax.nn.softmax(s, axis=-1)
        a = (p[:, :, 0] - lam * p[:, :, 1]).astype(v.dtype)
        return jnp.einsum('bhqk,bkhe->bqhe', a, v)

    o = lax.map(one_block, (q_blocks, jnp.arange(n_blocks)))
    o = o.transpose(1, 0, 2, 3, 4).reshape(B, S, H, 2 * D)
    o = rms_norm(o, subln_g) * (1.0 - lam_init)
    return o.reshape(B, S, H * 2 * D)


def hybrid_layer(x, w_in, w_out, rel_bias, lq1, lk1, lq2, lk2, subln_g, ln_g, ln_b, layer_idx):
    B, S, _ = x.shape
    h = jnp.einsum('bsd,dc->bsc', x, w_in)
    h_a, h_b = h[..., :4 * WIDTH_A], h[..., 4 * WIDTH_A:]
    qa, ka, va, ga = jnp.split(h_a, 4, axis=-1)
    qb, kb, vb, gb = jnp.split(h_b, 4, axis=-1)

    shp_a = (B, S, N_HEADS_A, HEAD_DIM)
    y_a = chunked_relpos_attention(qa.reshape(shp_a), ka.reshape(shp_a), va.reshape(shp_a), rel_bias)
    y_a = y_a * jax.nn.silu(ga)

    pos = jnp.arange(S, dtype=jnp.float32)
    shp_b = (B, S, N_HEADS_B, 2, HEAD_DIM)
    qb = rope(qb.reshape(shp_b), pos)
    kb = rope(kb.reshape(shp_b), pos)
    vb = vb.reshape(B, S, N_HEADS_B, 2 * HEAD_DIM)
    lam_init = 0.8 - 0.6 * math.exp(-0.3 * layer_idx)
    lam = (jnp.exp(jnp.sum(lq1.astype(jnp.float32) * lk1.astype(jnp.float32)))
           - jnp.exp(jnp.sum(lq2.astype(jnp.float32) * lk2.astype(jnp.float32))) + lam_init)
    y_b = differential_attention(qb, kb, vb, lam, subln_g, lam_init)
    y_b = y_b * jax.nn.silu(gb)

    y = jnp.einsum('bsc,cd->bsd', jnp.concatenate([y_a, y_b], axis=-1), w_out)
    return layer_norm(DEEPNORM_ALPHA * x + y, ln_g, ln_b)


def setup_inputs(seed: int = 0) -> dict:
    key = jax.random.key(seed)
    ks = jax.random.split(key, 12)
    x = jax.random.normal(ks[0], (BATCH, SEQ, D_MODEL), jnp.float32)
    col_scale = np.ones((PROJ_COLS,), np.float32)
    col_scale[2 * WIDTH_A:3 * WIDTH_A] = DEEPNORM_BETA
    col_scale[4 * WIDTH_A + 2 * WIDTH_B:4 * WIDTH_A + 3 * WIDTH_B] = DEEPNORM_BETA
    w_in = jax.random.normal(ks[1], (DEPTH, D_MODEL, PROJ_COLS), jnp.float32) * (D_MODEL ** -0.5) * jnp.asarray(col_scale)
    w_out = jax.random.normal(ks[2], (DEPTH, D_MODEL, D_MODEL), jnp.float32) * (D_MODEL ** -0.5) * DEEPNORM_BETA
    rel_bias = jax.random.normal(ks[3], (DEPTH, N_HEADS_A, 2 * REL_CLIP + 1), jnp.float32) * 0.1
    lambda_q1 = jax.random.normal(ks[4], (DEPTH, HEAD_DIM), jnp.float32) * 0.1
    lambda_k1 = jax.random.normal(ks[5], (DEPTH, HEAD_DIM), jnp.float32) * 0.1
    lambda_q2 = jax.random.normal(ks[6], (DEPTH, HEAD_DIM), jnp.float32) * 0.1
    lambda_k2 = jax.random.normal(ks[7], (DEPTH, HEAD_DIM), jnp.float32) * 0.1
    subln_g = 1.0 + 0.02 * jax.random.normal(ks[8], (DEPTH, 2 * HEAD_DIM), jnp.float32)
    ln_g = 1.0 + 0.02 * jax.random.normal(ks[9], (DEPTH, D_MODEL), jnp.float32)
    ln_b = 0.02 * jax.random.normal(ks[10], (DEPTH, D_MODEL), jnp.float32)
    return {"x": x, "w_in": w_in, "w_out": w_out, "rel_bias": rel_bias,
            "lambda_q1": lambda_q1, "lambda_k1": lambda_k1, "lambda_q2": lambda_q2, "lambda_k2": lambda_k2,
            "subln_g": subln_g, "ln_g": ln_g, "ln_b": ln_b}


def reference(x, w_in, w_out, rel_bias, lambda_q1, lambda_k1, lambda_q2, lambda_k2, subln_g, ln_g, ln_b):
    for l in range(DEPTH):
        x = hybrid_layer(x, w_in[l], w_out[l], rel_bias[l], lambda_q1[l], lambda_k1[l],
                         lambda_q2[l], lambda_k2[l], subln_g[l], ln_g[l], ln_b[l], l)
    return x
```

```python
import functools
import math

import numpy as np
import jax
import jax.numpy as jnp
from jax import lax
from jax.experimental import pallas as pl
from jax.experimental.pallas import tpu as pltpu

CHUNK = 64
N_LEFT_CHUNKS = 8
HEAD_DIM = 64
REL_CLIP = 128
ROPE_THETA = 10000.0
LN_EPS = 1e-5
RMS_EPS = 1e-5
NEG_INF = -1e30

LANES = 128
VMEM_LIMIT_BYTES = 52 * 1024 * 1024

PROJ_ROWS = 512
PROJ_COLS_CHUNK = 512
A_Q = 128
A_WIN = A_Q + N_LEFT_CHUNKS * CHUNK
A_VARIANTS = N_LEFT_CHUNKS * CHUNK // A_Q + 1
B_Q = 256
B_K = 256

_NT = (((1,), (1,)), ((), ()))


def _silu(g):
    return g * (1.0 / (1.0 + jnp.exp(-g)))


def _in_proj_kernel(x_ref, w_ref, wvt_ref, cos_ref, sina_ref, sinb_ref,
                    q_ref, k_ref, g_ref, vt_ref, *, width_a):
    xb = x_ref[0].astype(jnp.bfloat16)
    n_std = w_ref.shape[1] // PROJ_COLS_CHUNK
    per_kind = n_std // 3
    outs = (q_ref, k_ref, g_ref)
    cos = cos_ref[...]
    sina = sina_ref[...]
    sinb = sinb_ref[...]
    for c in range(n_std):
        kind, sub = divmod(c, per_kind)
        col0 = sub * PROJ_COLS_CHUNK
        r = jnp.dot(xb, w_ref[:, c * PROJ_COLS_CHUNK:(c + 1) * PROJ_COLS_CHUNK],
                    preferred_element_type=jnp.float32)
        rotary = kind < 2 and col0 >= width_a
        for t in range(PROJ_COLS_CHUNK // LANES):
            rt = r[:, t * LANES:(t + 1) * LANES]
            if rotary:
                rt = (rt * cos + pltpu.roll(rt, LANES - HEAD_DIM // 2, 1) * sina
                      + pltpu.roll(rt, HEAD_DIM // 2, 1) * sinb)
            outs[kind][0, :, col0 + t * LANES:col0 + (t + 1) * LANES] = rt.astype(jnp.bfloat16)
    rows = xb.shape[0]
    n_v = wvt_ref.shape[0] // PROJ_COLS_CHUNK
    for c in range(n_v):
        rt = lax.dot_general(wvt_ref[c * PROJ_COLS_CHUNK:(c + 1) * PROJ_COLS_CHUNK, :], xb, _NT,
                             preferred_element_type=jnp.float32)
        for j in range(rows // LANES):
            vt_ref[0, j, c * PROJ_COLS_CHUNK:(c + 1) * PROJ_COLS_CHUNK, :] = (
                rt[:, j * LANES:(j + 1) * LANES].astype(jnp.bfloat16))


def _in_proj(x, w_std, w_vt, cos, sina, sinb, width_a):
    B, S, D = x.shape
    n3 = w_std.shape[1]
    width = n3 // 3
    grid = (B, S // PROJ_ROWS)
    row_block = pl.BlockSpec((1, PROJ_ROWS, width), lambda b, i: (b, i, 0))
    tab_block = pl.BlockSpec((PROJ_ROWS, LANES), lambda b, i: (i, 0))
    return pl.pallas_call(
        functools.partial(_in_proj_kernel, width_a=width_a),
        grid=grid,
        in_specs=[
            pl.BlockSpec((1, PROJ_ROWS, D), lambda b, i: (b, i, 0)),
            pl.BlockSpec((D, n3), lambda b, i: (0, 0)),
            pl.BlockSpec((width, D), lambda b, i: (0, 0)),
            tab_block, tab_block, tab_block,
        ],
        out_specs=[
            row_block, row_block, row_block,
            pl.BlockSpec((1, PROJ_ROWS // LANES, width, LANES), lambda b, i: (b, i, 0, 0)),
        ],
        out_shape=[
            jax.ShapeDtypeStruct((B, S, width), jnp.bfloat16),
            jax.ShapeDtypeStruct((B, S, width), jnp.bfloat16),
            jax.ShapeDtypeStruct((B, S, width), jnp.bfloat16),
            jax.ShapeDtypeStruct((B, S // LANES, width, LANES), jnp.bfloat16),
        ],
        compiler_params=pltpu.CompilerParams(
            dimension_semantics=("arbitrary", "arbitrary"), vmem_limit_bytes=VMEM_LIMIT_BYTES),
    )(x, w_std, w_vt, cos, sina, sinb)


def _attn_a_kernel(q_ref, k_ref, vt_ref, g_ref, bias_ref, o_ref):
    i = pl.program_id(1)
    variant = jnp.minimum(i, A_VARIANTS - 1)
    blk0 = jnp.maximum(i - (A_VARIANTS - 1), 0)
    start = pl.multiple_of(blk0 * A_Q, A_Q)
    n_pairs = q_ref.shape[2] // LANES
    lane = lax.broadcasted_iota(jnp.int32, (A_Q, LANES), 1)
    first = lane < HEAD_DIM
    for p in range(n_pairs):
        cols = slice(p * LANES, (p + 1) * LANES)
        qp = q_ref[0, :, cols]
        zero = jnp.zeros_like(qp)
        q2 = jnp.concatenate([jnp.where(first, qp, zero), jnp.where(first, zero, qp)], axis=0)
        kw = k_ref[0, pl.ds(start, A_WIN), cols]
        s = lax.dot_general(kw, q2, _NT, preferred_element_type=jnp.float32)
        s = s + bias_ref[variant, p]
        m = jnp.max(s, axis=0, keepdims=True)
        e = jnp.exp(s - m)
        l = jnp.sum(e, axis=0, keepdims=True)
        vw = jnp.concatenate([vt_ref[0, blk0 + t, cols, :] for t in range(A_WIN // LANES)], axis=1)
        r = jnp.dot(vw, e.astype(jnp.bfloat16), preferred_element_type=jnp.float32)
        r = r * (1.0 / l)
        ot = jnp.concatenate([r[:HEAD_DIM, :A_Q], r[HEAD_DIM:, A_Q:]], axis=0)
        g = g_ref[0, :, cols].astype(jnp.float32)
        o_ref[0, :, cols] = (ot.T * _silu(g)).astype(o_ref.dtype)


def _attn_a(q_all, k_all, vt_all, g_all, bias, width_a):
    B, S, _ = q_all.shape
    grid = (B, S // A_Q)
    q_block = pl.BlockSpec((1, A_Q, width_a), lambda b, i: (b, i, 0))
    return pl.pallas_call(
        _attn_a_kernel,
        grid=grid,
        in_specs=[
            q_block,
            pl.BlockSpec((1, S, width_a), lambda b, i: (b, 0, 0)),
            pl.BlockSpec((1, S // LANES, width_a, LANES), lambda b, i: (b, 0, 0, 0)),
            q_block,
            pl.BlockSpec(bias.shape, lambda b, i: (0, 0, 0, 0), pipeline_mode=pl.Buffered(1)),
        ],
        out_specs=q_block,
        out_shape=jax.ShapeDtypeStruct((B, S, width_a), jnp.bfloat16),
        compiler_params=pltpu.CompilerParams(
            dimension_semantics=("arbitrary", "arbitrary"), vmem_limit_bytes=VMEM_LIMIT_BYTES),
    )(q_all, k_all, vt_all, g_all, bias)


def _bias_tables(rel_bias):
    n_heads = rel_bias.shape[0]
    v = np.arange(A_VARIANTS)[:, None, None]
    kk = np.arange(A_WIN)[None, :, None]
    r = np.arange(A_Q)[None, None, :]
    qpos = v * A_Q + r
    dchunk = qpos // CHUNK - kk // CHUNK
    valid = (dchunk >= 0) & (dchunk <= N_LEFT_CHUNKS)
    idx = np.clip(qpos - kk, -REL_CLIP, REL_CLIP) + REL_CLIP
    b = rel_bias.astype(jnp.float32)[:, idx]
    b = jnp.where(valid[None], b, NEG_INF)
    b = b.reshape(n_heads // 2, 2, A_VARIANTS, A_WIN, A_Q).transpose(2, 0, 3, 1, 4)
    return b.reshape(A_VARIANTS, n_heads // 2, A_WIN, 2 * A_Q)


def _attn_b_kernel(q_ref, k_ref, vt_ref, g_ref, lq1_ref, lk1_ref, lq2_ref, lk2_ref, sub_ref,
                   o_ref, acc_ref, *, lam_init):
    i = pl.program_id(2)
    lane = lax.broadcasted_iota(jnp.int32, (B_Q, LANES), 1)
    first = lane < HEAD_DIM
    qp = q_ref[0]
    zero = jnp.zeros_like(qp)
    qs = (jnp.where(first, qp, zero), jnp.where(first, zero, qp))
    acc_ref[...] = jnp.zeros_like(acc_ref)

    def step(j, carry, masked):
        kj = k_ref[0, pl.ds(pl.multiple_of(j * B_K, B_K), B_K), :]
        vj = jnp.concatenate([vt_ref[0, j * (B_K // LANES) + t] for t in range(B_K // LANES)],
                             axis=1)
        new = []
        for n in range(2):
            m, l = carry[2 * n], carry[2 * n + 1]
            s = lax.dot_general(kj, qs[n], _NT, preferred_element_type=jnp.float32)
            if masked:
                kc = lax.broadcasted_iota(jnp.int32, (B_K, B_Q), 0) // CHUNK
                qc = lax.broadcasted_iota(jnp.int32, (B_K, B_Q), 1) // CHUNK
                s = jnp.where(kc <= qc, s, NEG_INF)
            m_new = jnp.maximum(m, jnp.max(s, axis=0, keepdims=True))
            alpha = jnp.exp(m - m_new)
            e = jnp.exp(s - m_new)
            l = alpha * l + jnp.sum(e, axis=0, keepdims=True)
            acc_ref[n] = alpha * acc_ref[n] + jnp.dot(vj, e.astype(jnp.bfloat16),
                                                      preferred_element_type=jnp.float32)
            new += [m_new, l]
        return tuple(new)

    init_m = jnp.full((1, B_Q), NEG_INF, jnp.float32)
    init_l = jnp.zeros((1, B_Q), jnp.float32)
    carry = lax.fori_loop(0, i, functools.partial(step, masked=False),
                          (init_m, init_l, init_m, init_l))
    _, l0, _, l1 = step(i, carry, masked=True)

    lam = (jnp.exp(jnp.sum(lq1_ref[...] * lk1_ref[...], axis=1, keepdims=True))
           - jnp.exp(jnp.sum(lq2_ref[...] * lk2_ref[...], axis=1, keepdims=True)) + lam_init)
    ot = acc_ref[0] * (1.0 / l0) - lam * (acc_ref[1] * (1.0 / l1))
    o = ot.T
    ms = jnp.mean(o * o, axis=1, keepdims=True)
    o = o * lax.rsqrt(ms + RMS_EPS) * sub_ref[...] * (1.0 - lam_init)
    g = g_ref[0].astype(jnp.float32)
    o_ref[0] = (o * _silu(g)).astype(o_ref.dtype)


def _attn_b(q_all, k_all, vt_all, g_all, lq1, lk1, lq2, lk2, subln, width_a, lam_init):
    B, S, width = q_all.shape
    n_heads = (width - width_a) // LANES
    col0 = width_a // LANES
    grid = (B, n_heads, S // B_Q)
    q_block = pl.BlockSpec((1, B_Q, LANES), lambda b, h, i: (b, i, col0 + h))
    vec64 = pl.BlockSpec((1, HEAD_DIM), lambda b, h, i: (0, 0))
    return pl.pallas_call(
        functools.partial(_attn_b_kernel, lam_init=lam_init),
        grid=grid,
        in_specs=[
            q_block,
            pl.BlockSpec((1, S, LANES), lambda b, h, i: (b, 0, col0 + h)),
            pl.BlockSpec((1, S // LANES, LANES, LANES), lambda b, h, i: (b, 0, col0 + h, 0)),
            q_block,
            vec64, vec64, vec64, vec64,
            pl.BlockSpec((1, LANES), lambda b, h, i: (0, 0)),
        ],
        out_specs=pl.BlockSpec((1, B_Q, LANES), lambda b, h, i: (b, i, h)),
        out_shape=jax.ShapeDtypeStruct((B, S, width - width_a), jnp.bfloat16),
        scratch_shapes=[pltpu.VMEM((2, LANES, B_Q), jnp.float32)],
        compiler_params=pltpu.CompilerParams(
            dimension_semantics=("arbitrary", "arbitrary", "arbitrary"),
            vmem_limit_bytes=VMEM_LIMIT_BYTES),
    )(q_all, k_all, vt_all, g_all, lq1, lk1, lq2, lk2, subln)


def _out_proj_kernel(x_ref, ya_ref, yb_ref, wa_ref, wb_ref, g_ref, b_ref, o_ref, *, alpha):
    y = jnp.dot(ya_ref[0], wa_ref[...], preferred_element_type=jnp.float32)
    y = y + jnp.dot(yb_ref[0], wb_ref[...], preferred_element_type=jnp.float32)
    z = alpha * x_ref[0] + y
    mu = jnp.mean(z, axis=1, keepdims=True)
    zc = z - mu
    var = jnp.mean(zc * zc, axis=1, keepdims=True)
    o_ref[0] = zc * lax.rsqrt(var + LN_EPS) * g_ref[...] + b_ref[...]


def _out_proj(x, ya, yb, wa, wb, ln_g, ln_b, alpha):
    B, S, D = x.shape
    grid = (B, S // PROJ_ROWS)
    x_block = pl.BlockSpec((1, PROJ_ROWS, D), lambda b, i: (b, i, 0))
    vec = pl.BlockSpec((1, D), lambda b, i: (0, 0))
    return pl.pallas_call(
        functools.partial(_out_proj_kernel, alpha=alpha),
        grid=grid,
        in_specs=[
            x_block,
            pl.BlockSpec((1, PROJ_ROWS, ya.shape[2]), lambda b, i: (b, i, 0)),
            pl.BlockSpec((1, PROJ_ROWS, yb.shape[2]), lambda b, i: (b, i, 0)),
            pl.BlockSpec(wa.shape, lambda b, i: (0, 0)),
            pl.BlockSpec(wb.shape, lambda b, i: (0, 0)),
            vec, vec,
        ],
        out_specs=x_block,
        out_shape=jax.ShapeDtypeStruct((B, S, D), jnp.float32),
        compiler_params=pltpu.CompilerParams(
            dimension_semantics=("arbitrary", "arbitrary"), vmem_limit_bytes=VMEM_LIMIT_BYTES),
    )(x, ya, yb, wa, wb, ln_g, ln_b)


def _rope_tables(seq):
    half = HEAD_DIM // 2
    inv_freq = ROPE_THETA ** (-jnp.arange(0, HEAD_DIM, 2, dtype=jnp.float32) / HEAD_DIM)
    pos = jnp.arange(seq, dtype=jnp.float32)
    ang = pos[:, None] * inv_freq[None, :]
    ang = jnp.concatenate([ang] * (LANES // half), axis=-1)
    low = (np.arange(LANES) % HEAD_DIM) < half
    sin = jnp.sin(ang)
    return jnp.cos(ang), jnp.where(low, -sin, 0.0), jnp.where(low, 0.0, sin)


def kernel(x, w_in, w_out, rel_bias, lambda_q1, lambda_k1, lambda_q2, lambda_k2, subln_g, ln_g, ln_b):
    depth, d_model, proj_cols = w_in.shape
    seq = x.shape[1]
    width_a = proj_cols // 8
    alpha = (2 * depth) ** 0.25
    scale = HEAD_DIM ** -0.5
    cos, sina, sinb = _rope_tables(seq)
    for l in range(depth):
        w = w_in[l]
        cols = [w[:, j * width_a:(j + 1) * width_a] for j in range(8)]
        qa, ka, va, ga, qb, kb, vb, gb = cols
        w_std = jnp.concatenate([qa * scale, qb * scale, ka, kb, ga, gb], axis=1).astype(jnp.bfloat16)
        w_vt = jnp.concatenate([va, vb], axis=1).T.astype(jnp.bfloat16)
        q_all, k_all, g_all, vt_all = _in_proj(x, w_std, w_vt, cos, sina, sinb, width_a)

        ya = _attn_a(q_all, k_all, vt_all, g_all, _bias_tables(rel_bias[l]), width_a)
        lam_init = 0.8 - 0.6 * math.exp(-0.3 * l)
        yb = _attn_b(q_all, k_all, vt_all, g_all,
                     lambda_q1[l][None].astype(jnp.float32), lambda_k1[l][None].astype(jnp.float32),
                     lambda_q2[l][None].astype(jnp.float32), lambda_k2[l][None].astype(jnp.float32),
                     subln_g[l][None].astype(jnp.float32), width_a, lam_init)

        wo = w_out[l].astype(jnp.bfloat16)
        x = _out_proj(x, ya, yb, wo[:width_a], wo[width_a:], ln_g[l][None], ln_b[l][None], alpha)
    return x
```

```python
import functools
import math

import numpy as np
import jax
import jax.numpy as jnp
from jax import lax
from jax.experimental import pallas as pl
from jax.experimental.pallas import tpu as pltpu

CHUNK = 64
N_LEFT_CHUNKS = 8
HEAD_DIM = 64
REL_CLIP = 128
ROPE_THETA = 10000.0
LN_EPS = 1e-5
RMS_EPS = 1e-5
NEG_INF = -1e30

LANES = 128
VMEM_LIMIT_BYTES = 52 * 1024 * 1024

PROJ_ROWS = 512
PROJ_COLS_CHUNK = 512
A_Q = 128
A_WIN = A_Q + N_LEFT_CHUNKS * CHUNK
A_LEFT_BLOCKS = N_LEFT_CHUNKS * CHUNK // A_Q
A_BIAS_ROWS = A_WIN + A_LEFT_BLOCKS * A_Q
B_Q = 512
B_K = 256
BF16_SUBLANES = 16
B_ACC_ROWS = LANES + BF16_SUBLANES
LOG2E = 1.4426950408889634

_NT = (((1,), (1,)), ((), ()))


def _silu(g):
    return g * (1.0 / (1.0 + jnp.exp(-g)))


def _in_proj_kernel(x_ref, w_ref, wvt_ref, cos_ref, sina_ref, sinb_ref,
                    q_ref, k_ref, g_ref, vt_ref, *, width_a):
    xb = x_ref[0].astype(jnp.bfloat16)
    n_std = w_ref.shape[1] // PROJ_COLS_CHUNK
    per_kind = n_std // 3
    outs = (q_ref, k_ref, g_ref)
    cos = cos_ref[...]
    sina = sina_ref[...]
    sinb = sinb_ref[...]
    for c in range(n_std):
        kind, sub = divmod(c, per_kind)
        col0 = sub * PROJ_COLS_CHUNK
        r = jnp.dot(xb, w_ref[:, c * PROJ_COLS_CHUNK:(c + 1) * PROJ_COLS_CHUNK],
                    preferred_element_type=jnp.float32)
        rotary = kind < 2 and col0 >= width_a
        for t in range(PROJ_COLS_CHUNK // LANES):
            rt = r[:, t * LANES:(t + 1) * LANES]
            if rotary:
                rt = (rt * cos + pltpu.roll(rt, LANES - HEAD_DIM // 2, 1) * sina
                      + pltpu.roll(rt, HEAD_DIM // 2, 1) * sinb)
            outs[kind][0, :, col0 + t * LANES:col0 + (t + 1) * LANES] = rt.astype(jnp.bfloat16)
    rows = xb.shape[0]
    n_v = wvt_ref.shape[0] // PROJ_COLS_CHUNK
    for c in range(n_v):
        rt = lax.dot_general(wvt_ref[c * PROJ_COLS_CHUNK:(c + 1) * PROJ_COLS_CHUNK, :], xb, _NT,
                             preferred_element_type=jnp.float32)
        for j in range(rows // LANES):
            vt_ref[0, j, c * PROJ_COLS_CHUNK:(c + 1) * PROJ_COLS_CHUNK, :] = (
                rt[:, j * LANES:(j + 1) * LANES].astype(jnp.bfloat16))


def _in_proj(x, w_std, w_vt, cos, sina, sinb, width_a):
    B, S, D = x.shape
    n3 = w_std.shape[1]
    width = n3 // 3
    grid = (B, S // PROJ_ROWS)
    row_block = pl.BlockSpec((1, PROJ_ROWS, width), lambda b, i: (b, i, 0))
    tab_block = pl.BlockSpec((PROJ_ROWS, LANES), lambda b, i: (i, 0))
    return pl.pallas_call(
        functools.partial(_in_proj_kernel, width_a=width_a),
        grid=grid,
        in_specs=[
            pl.BlockSpec((1, PROJ_ROWS, D), lambda b, i: (b, i, 0)),
            pl.BlockSpec((D, n3), lambda b, i: (0, 0)),
            pl.BlockSpec((width, D), lambda b, i: (0, 0)),
            tab_block, tab_block, tab_block,
        ],
        out_specs=[
            row_block, row_block, row_block,
            pl.BlockSpec((1, PROJ_ROWS // LANES, width, LANES), lambda b, i: (b, i, 0, 0)),
        ],
        out_shape=[
            jax.ShapeDtypeStruct((B, S, width), jnp.bfloat16),
            jax.ShapeDtypeStruct((B, S, width), jnp.bfloat16),
            jax.ShapeDtypeStruct((B, S, width), jnp.bfloat16),
            jax.ShapeDtypeStruct((B, S // LANES, width, LANES), jnp.bfloat16),
        ],
        compiler_params=pltpu.CompilerParams(
            dimension_semantics=("arbitrary", "arbitrary"), vmem_limit_bytes=VMEM_LIMIT_BYTES),
    )(x, w_std, w_vt, cos, sina, sinb)


def _attn_a_kernel(q_ref, k_ref, vt_ref, g_ref, bias_ref, o_ref):
    i = pl.program_id(1)
    blk0 = jnp.maximum(i - A_LEFT_BLOCKS, 0)
    start = pl.multiple_of(blk0 * A_Q, A_Q)
    bias_row = pl.multiple_of((A_LEFT_BLOCKS - (i - blk0)) * A_Q, A_Q)
    n_pairs = q_ref.shape[2] // LANES
    lane = lax.broadcasted_iota(jnp.int32, (A_Q, LANES), 1)
    first = lane < HEAD_DIM
    for p in range(n_pairs):
        cols = slice(p * LANES, (p + 1) * LANES)
        qp = q_ref[0, :, cols]
        zero = jnp.zeros_like(qp)
        q2 = jnp.concatenate([jnp.where(first, qp, zero), jnp.where(first, zero, qp)], axis=0)
        kw = k_ref[0, pl.ds(start, A_WIN), cols]
        s = lax.dot_general(kw, q2, _NT, preferred_element_type=jnp.float32)
        s = s + bias_ref[p, pl.ds(bias_row, A_WIN), :]
        m = jnp.max(s, axis=0, keepdims=True)
        e = jnp.exp(s - m)
        l = jnp.sum(e, axis=0, keepdims=True)
        vw = jnp.concatenate([vt_ref[0, blk0 + t, cols, :] for t in range(A_WIN // LANES)], axis=1)
        r = jnp.dot(vw, e.astype(jnp.bfloat16), preferred_element_type=jnp.float32)
        r = r * (1.0 / l)
        ot = jnp.concatenate([r[:HEAD_DIM, :A_Q], r[HEAD_DIM:, A_Q:]], axis=0)
        g = g_ref[0, :, cols].astype(jnp.float32)
        o_ref[0, :, cols] = (ot.T * _silu(g)).astype(o_ref.dtype)


def _attn_a(q_all, k_all, vt_all, g_all, bias, width_a):
    B, S, _ = q_all.shape
    grid = (B, S // A_Q)
    q_block = pl.BlockSpec((1, A_Q, width_a), lambda b, i: (b, i, 0))
    return pl.pallas_call(
        _attn_a_kernel,
        grid=grid,
        in_specs=[
            q_block,
            pl.BlockSpec((1, S, width_a), lambda b, i: (b, 0, 0)),
            pl.BlockSpec((1, S // LANES, width_a, LANES), lambda b, i: (b, 0, 0, 0)),
            q_block,
            pl.BlockSpec(bias.shape, lambda b, i: (0, 0, 0)),
        ],
        out_specs=q_block,
        out_shape=jax.ShapeDtypeStruct((B, S, width_a), jnp.bfloat16),
        compiler_params=pltpu.CompilerParams(
            dimension_semantics=("arbitrary", "arbitrary"), vmem_limit_bytes=VMEM_LIMIT_BYTES),
    )(q_all, k_all, vt_all, g_all, bias)


def _bias_table(rel_bias):
    n_heads = rel_bias.shape[0]
    n_blk = A_BIAS_ROWS // A_Q
    e = np.arange(2 * A_Q)
    d = np.where(e < A_Q, e, e - 2 * A_Q)
    a = np.arange(n_blk)[:, None]
    idx = np.clip(A_LEFT_BLOCKS * A_Q - A_Q * a + d[None, :], -REL_CLIP, REL_CLIP) + REL_CLIP
    f = rel_bias.astype(jnp.float32)[:, idx]
    t = jnp.tile(f, (1, 1, A_Q))[..., :A_Q * (2 * A_Q - 1)]
    t = t.reshape(n_heads, n_blk, A_Q, 2 * A_Q - 1)[..., :A_Q]
    t = t.reshape(n_heads, A_BIAS_ROWS, A_Q)
    kk = np.arange(A_BIAS_ROWS)[:, None]
    r = np.arange(A_Q)[None, :]
    dchunk = r // CHUNK - kk // CHUNK + N_LEFT_CHUNKS
    valid = (dchunk >= 0) & (dchunk <= N_LEFT_CHUNKS)
    t = jnp.where(valid[None], t, NEG_INF)
    t = t.reshape(n_heads // 2, 2, A_BIAS_ROWS, A_Q).transpose(0, 2, 1, 3)
    return t.reshape(n_heads // 2, A_BIAS_ROWS, 2 * A_Q)


def _attn_b_kernel(q_ref, k_ref, vt_ref, g_ref, lq1_ref, lk1_ref, lq2_ref, lk2_ref, sub_ref,
                   o_ref, qm_ref, s_ref, acc_ref, *, lam_init):
    i = pl.program_id(2)
    lane = lax.broadcasted_iota(jnp.int32, (B_Q, LANES), 1)
    first = lane < HEAD_DIM
    qp = q_ref[0]
    zero = jnp.zeros_like(qp)
    qm_ref[0] = jnp.where(first, qp, zero)
    qm_ref[1] = jnp.where(first, zero, qp)
    acc_ref[...] = jnp.zeros_like(acc_ref)
    ones = jnp.ones((BF16_SUBLANES, B_K), jnp.bfloat16)
    blocks_per_step = B_K // LANES

    def scores(j, slot):
        kj = k_ref[0, pl.ds(pl.multiple_of(j * B_K, B_K), B_K), :]
        for n in range(2):
            s_ref[slot, n] = lax.dot_general(kj, qm_ref[n], _NT,
                                             preferred_element_type=jnp.float32)

    def accumulate(j, slot, ms, mask=None):
        vj = jnp.concatenate([vt_ref[0, j * blocks_per_step + t] for t in range(blocks_per_step)],
                             axis=1)
        vj = jnp.concatenate([vj, ones], axis=0)
        new = []
        for n in range(2):
            s = s_ref[slot, n]
            if mask is not None:
                s = jnp.where(mask, s, NEG_INF)
            m_new = jnp.maximum(ms[n], jnp.max(s, axis=0, keepdims=True))
            alpha = jnp.exp2(ms[n] - m_new)
            p = jnp.exp2(s - m_new).astype(jnp.bfloat16)
            acc_ref[n] = alpha * acc_ref[n] + jnp.dot(vj, p, preferred_element_type=jnp.float32)
            new.append(m_new)
        return tuple(new)

    def two_blocks(jj, ms):
        j = 2 * jj
        scores(j + 1, 1)
        ms = accumulate(j, 0, ms)
        scores(j + 2, 0)
        return accumulate(j + 1, 1, ms)

    init_m = jnp.full((1, B_Q), NEG_INF, jnp.float32)
    scores(0, 0)
    n_full = i * (B_Q // B_K)
    ms = lax.fori_loop(0, n_full // 2, two_blocks, (init_m, init_m))
    kc = lax.broadcasted_iota(jnp.int32, (B_K, B_Q), 0) // CHUNK
    qc = lax.broadcasted_iota(jnp.int32, (B_K, B_Q), 1) // CHUNK
    scores(n_full + 1, 1)
    ms = accumulate(n_full, 0, ms, kc <= qc)
    accumulate(n_full + 1, 1, ms, kc + B_K // CHUNK <= qc)

    lam = (jnp.exp(jnp.sum(lq1_ref[...] * lk1_ref[...], axis=1, keepdims=True))
           - jnp.exp(jnp.sum(lq2_ref[...] * lk2_ref[...], axis=1, keepdims=True)) + lam_init)
    l0 = acc_ref[0, LANES:LANES + 1, :]
    l1 = acc_ref[1, LANES:LANES + 1, :]
    ot = acc_ref[0, :LANES, :] * (1.0 / l0) - lam * (acc_ref[1, :LANES, :] * (1.0 / l1))
    o = ot.T
    ms = jnp.mean(o * o, axis=1, keepdims=True)
    o = o * lax.rsqrt(ms + RMS_EPS) * sub_ref[...] * (1.0 - lam_init)
    g = g_ref[0].astype(jnp.float32)
    o_ref[0] = (o * _silu(g)).astype(o_ref.dtype)


def _attn_b(q_all, k_all, vt_all, g_all, lq1, lk1, lq2, lk2, subln, width_a, lam_init):
    B, S, width = q_all.shape
    n_heads = (width - width_a) // LANES
    col0 = width_a // LANES
    grid = (B, n_heads, S // B_Q)
    q_block = pl.BlockSpec((1, B_Q, LANES), lambda b, h, i: (b, i, col0 + h))
    vec64 = pl.BlockSpec((1, HEAD_DIM), lambda b, h, i: (0, 0))
    return pl.pallas_call(
        functools.partial(_attn_b_kernel, lam_init=lam_init),
        grid=grid,
        in_specs=[
            q_block,
            pl.BlockSpec((1, S, LANES), lambda b, h, i: (b, 0, col0 + h)),
            pl.BlockSpec((1, S // LANES, LANES, LANES), lambda b, h, i: (b, 0, col0 + h, 0)),
            q_block,
            vec64, vec64, vec64, vec64,
            pl.BlockSpec((1, LANES), lambda b, h, i: (0, 0)),
        ],
        out_specs=pl.BlockSpec((1, B_Q, LANES), lambda b, h, i: (b, i, h)),
        out_shape=jax.ShapeDtypeStruct((B, S, width - width_a), jnp.bfloat16),
        scratch_shapes=[
            pltpu.VMEM((2, B_Q, LANES), jnp.bfloat16),
            pltpu.VMEM((2, 2, B_K, B_Q), jnp.float32),
            pltpu.VMEM((2, B_ACC_ROWS, B_Q), jnp.float32),
        ],
        compiler_params=pltpu.CompilerParams(
            dimension_semantics=("arbitrary", "arbitrary", "arbitrary"),
            vmem_limit_bytes=VMEM_LIMIT_BYTES),
    )(q_all, k_all, vt_all, g_all, lq1, lk1, lq2, lk2, subln)


def _out_proj_kernel(x_ref, ya_ref, yb_ref, wa_ref, wb_ref, g_ref, b_ref, o_ref, *, alpha):
    y = jnp.dot(ya_ref[0], wa_ref[...], preferred_element_type=jnp.float32)
    y = y + jnp.dot(yb_ref[0], wb_ref[...], preferred_element_type=jnp.float32)
    z = alpha * x_ref[0] + y
    mu = jnp.mean(z, axis=1, keepdims=True)
    zc = z - mu
    var = jnp.mean(zc * zc, axis=1, keepdims=True)
    o_ref[0] = zc * lax.rsqrt(var + LN_EPS) * g_ref[...] + b_ref[...]


def _out_proj(x, ya, yb, wa, wb, ln_g, ln_b, alpha):
    B, S, D = x.shape
    grid = (B, S // PROJ_ROWS)
    x_block = pl.BlockSpec((1, PROJ_ROWS, D), lambda b, i: (b, i, 0))
    vec = pl.BlockSpec((1, D), lambda b, i: (0, 0))
    return pl.pallas_call(
        functools.partial(_out_proj_kernel, alpha=alpha),
        grid=grid,
        in_specs=[
            x_block,
            pl.BlockSpec((1, PROJ_ROWS, ya.shape[2]), lambda b, i: (b, i, 0)),
            pl.BlockSpec((1, PROJ_ROWS, yb.shape[2]), lambda b, i: (b, i, 0)),
            pl.BlockSpec(wa.shape, lambda b, i: (0, 0)),
            pl.BlockSpec(wb.shape, lambda b, i: (0, 0)),
            vec, vec,
        ],
        out_specs=x_block,
        out_shape=jax.ShapeDtypeStruct((B, S, D), jnp.float32),
        compiler_params=pltpu.CompilerParams(
            dimension_semantics=("arbitrary", "arbitrary"), vmem_limit_bytes=VMEM_LIMIT_BYTES),
    )(x, ya, yb, wa, wb, ln_g, ln_b)


def _rope_tables(seq):
    half = HEAD_DIM // 2
    inv_freq = ROPE_THETA ** (-jnp.arange(0, HEAD_DIM, 2, dtype=jnp.float32) / HEAD_DIM)
    pos = jnp.arange(seq, dtype=jnp.float32)
    ang = pos[:, None] * inv_freq[None, :]
    ang = jnp.concatenate([ang] * (LANES // half), axis=-1)
    low = (np.arange(LANES) % HEAD_DIM) < half
    sin = jnp.sin(ang)
    return jnp.cos(ang), jnp.where(low, -sin, 0.0), jnp.where(low, 0.0, sin)


def kernel(x, w_in, w_out, rel_bias, lambda_q1, lambda_k1, lambda_q2, lambda_k2, subln_g, ln_g, ln_b):
    depth, d_model, proj_cols = w_in.shape
    seq = x.shape[1]
    width_a = proj_cols // 8
    alpha = (2 * depth) ** 0.25
    scale = HEAD_DIM ** -0.5
    cos, sina, sinb = _rope_tables(seq)
    for l in range(depth):
        w = w_in[l]
        cols = [w[:, j * width_a:(j + 1) * width_a] for j in range(8)]
        qa, ka, va, ga, qb, kb, vb, gb = cols
        w_std = jnp.concatenate([qa * scale, qb * (scale * LOG2E), ka, kb, ga, gb], axis=1).astype(jnp.bfloat16)
        w_vt = jnp.concatenate([va, vb], axis=1).T.astype(jnp.bfloat16)
        q_all, k_all, g_all, vt_all = _in_proj(x, w_std, w_vt, cos, sina, sinb, width_a)

        ya = _attn_a(q_all, k_all, vt_all, g_all, _bias_table(rel_bias[l]), width_a)
        lam_init = 0.8 - 0.6 * math.exp(-0.3 * l)
        yb = _attn_b(q_all, k_all, vt_all, g_all,
                     lambda_q1[l][None].astype(jnp.float32), lambda_k1[l][None].astype(jnp.float32),
                     lambda_q2[l][None].astype(jnp.float32), lambda_k2[l][None].astype(jnp.float32),
                     subln_g[l][None].astype(jnp.float32), width_a, lam_init)

        wo = w_out[l].astype(jnp.bfloat16)
        x = _out_proj(x, ya, yb, wo[:width_a], wo[width_a:], ln_g[l][None], ln_b[l][None], alpha)
    return x
```

```python
import functools
import math

import numpy as np
import jax
import jax.numpy as jnp
from jax import lax
from jax.experimental import pallas as pl
from jax.experimental.pallas import tpu as pltpu

CHUNK = 64
N_LEFT_CHUNKS = 8
HEAD_DIM = 64
REL_CLIP = 128
ROPE_THETA = 10000.0
LN_EPS = 1e-5
RMS_EPS = 1e-5
NEG_INF = -1e30

LANES = 128
VMEM_LIMIT_BYTES = 52 * 1024 * 1024

PROJ_ROWS = 512
PROJ_COLS_CHUNK = 512
A_Q = 128
A_WIN = A_Q + N_LEFT_CHUNKS * CHUNK
A_LEFT_BLOCKS = N_LEFT_CHUNKS * CHUNK // A_Q
A_BIAS_ROWS = A_WIN + A_LEFT_BLOCKS * A_Q
A_STEP_BLOCKS = 2
B_Q = 512
B_K = 512
BF16_SUBLANES = 16
ACC_ROWS = LANES + BF16_SUBLANES
LOG2E = 1.4426950408889634

_NT = (((1,), (1,)), ((), ()))


def _silu(g):
    return g * (1.0 / (1.0 + jnp.exp(-g)))


def _in_proj_kernel(x_ref, w_ref, wvt_ref, cos_ref, sina_ref, sinb_ref,
                    q_ref, k_ref, g_ref, vt_ref, *, width_a):
    xb = x_ref[0].astype(jnp.bfloat16)
    n_std = w_ref.shape[1] // PROJ_COLS_CHUNK
    per_kind = n_std // 3
    outs = (q_ref, k_ref, g_ref)
    cos = cos_ref[...]
    sina = sina_ref[...]
    sinb = sinb_ref[...]
    for c in range(n_std):
        kind, sub = divmod(c, per_kind)
        col0 = sub * PROJ_COLS_CHUNK
        r = jnp.dot(xb, w_ref[:, c * PROJ_COLS_CHUNK:(c + 1) * PROJ_COLS_CHUNK],
                    preferred_element_type=jnp.float32)
        rotary = kind < 2 and col0 >= width_a
        for t in range(PROJ_COLS_CHUNK // LANES):
            rt = r[:, t * LANES:(t + 1) * LANES]
            if rotary:
                rt = (rt * cos + pltpu.roll(rt, LANES - HEAD_DIM // 2, 1) * sina
                      + pltpu.roll(rt, HEAD_DIM // 2, 1) * sinb)
            outs[kind][0, :, col0 + t * LANES:col0 + (t + 1) * LANES] = rt.astype(jnp.bfloat16)
    rows = xb.shape[0]
    n_v = wvt_ref.shape[0] // PROJ_COLS_CHUNK
    for c in range(n_v):
        rt = lax.dot_general(wvt_ref[c * PROJ_COLS_CHUNK:(c + 1) * PROJ_COLS_CHUNK, :], xb, _NT,
                             preferred_element_type=jnp.float32)
        for j in range(rows // LANES):
            vt_ref[0, j, c * PROJ_COLS_CHUNK:(c + 1) * PROJ_COLS_CHUNK, :] = (
                rt[:, j * LANES:(j + 1) * LANES].astype(jnp.bfloat16))


def _in_proj(x, w_std, w_vt, cos, sina, sinb, width_a):
    B, S, D = x.shape
    n3 = w_std.shape[1]
    width = n3 // 3
    grid = (B, S // PROJ_ROWS)
    row_block = pl.BlockSpec((1, PROJ_ROWS, width), lambda b, i: (b, i, 0))
    tab_block = pl.BlockSpec((PROJ_ROWS, LANES), lambda b, i: (i, 0))
    return pl.pallas_call(
        functools.partial(_in_proj_kernel, width_a=width_a),
        grid=grid,
        in_specs=[
            pl.BlockSpec((1, PROJ_ROWS, D), lambda b, i: (b, i, 0)),
            pl.BlockSpec((D, n3), lambda b, i: (0, 0)),
            pl.BlockSpec((width, D), lambda b, i: (0, 0)),
            tab_block, tab_block, tab_block,
        ],
        out_specs=[
            row_block, row_block, row_block,
            pl.BlockSpec((1, PROJ_ROWS // LANES, width, LANES), lambda b, i: (b, i, 0, 0)),
        ],
        out_shape=[
            jax.ShapeDtypeStruct((B, S, width), jnp.bfloat16),
            jax.ShapeDtypeStruct((B, S, width), jnp.bfloat16),
            jax.ShapeDtypeStruct((B, S, width), jnp.bfloat16),
            jax.ShapeDtypeStruct((B, S // LANES, width, LANES), jnp.bfloat16),
        ],
        compiler_params=pltpu.CompilerParams(
            dimension_semantics=("arbitrary", "arbitrary"), vmem_limit_bytes=VMEM_LIMIT_BYTES),
    )(x, w_std, w_vt, cos, sina, sinb)


def _attn_a_kernel(q_ref, k_ref, vt_ref, g_ref, bias_ref, o_ref, s_ref, p_ref):
    i = pl.program_id(1)
    n_pairs = q_ref.shape[2] // LANES
    lane = lax.broadcasted_iota(jnp.int32, (A_Q, LANES), 1)
    first = lane < HEAD_DIM
    ones = jnp.ones((BF16_SUBLANES, A_WIN), jnp.bfloat16)
    items = [(sb, p) for sb in range(A_STEP_BLOCKS) for p in range(n_pairs)]

    def first_key_block(sb):
        return jnp.maximum(i * A_STEP_BLOCKS + sb - A_LEFT_BLOCKS, 0)

    def scores(t):
        sb, p = items[t]
        cols = slice(p * LANES, (p + 1) * LANES)
        qp = q_ref[0, sb * A_Q:(sb + 1) * A_Q, cols]
        zero = jnp.zeros_like(qp)
        q2 = jnp.concatenate([jnp.where(first, qp, zero), jnp.where(first, zero, qp)], axis=0)
        start = pl.multiple_of(first_key_block(sb) * A_Q, A_Q)
        kw = k_ref[0, pl.ds(start, A_WIN), cols]
        s_ref[t % 2] = lax.dot_general(kw, q2, _NT, preferred_element_type=jnp.float32)

    def softmax(t):
        sb, p = items[t]
        blk0 = first_key_block(sb)
        bias_row = pl.multiple_of((A_LEFT_BLOCKS - (i * A_STEP_BLOCKS + sb - blk0)) * A_Q, A_Q)
        s = s_ref[t % 2] + bias_ref[p, pl.ds(bias_row, A_WIN), :]
        m = jnp.max(s, axis=0, keepdims=True)
        p_ref[t % 2] = jnp.exp2(s - m).astype(jnp.bfloat16)

    def finish(t):
        sb, p = items[t]
        cols = slice(p * LANES, (p + 1) * LANES)
        rows = slice(sb * A_Q, (sb + 1) * A_Q)
        blk0 = first_key_block(sb)
        vw = jnp.concatenate([vt_ref[0, blk0 + u, cols, :] for u in range(A_WIN // LANES)], axis=1)
        vw = jnp.concatenate([vw, ones], axis=0)
        r = jnp.dot(vw, p_ref[t % 2], preferred_element_type=jnp.float32)
        r = r[:LANES] * (1.0 / r[LANES:LANES + 1])
        ot = jnp.concatenate([r[:HEAD_DIM, :A_Q], r[HEAD_DIM:, A_Q:]], axis=0)
        g = g_ref[0, rows, cols].astype(jnp.float32)
        o_ref[0, rows, cols] = (ot.T * _silu(g)).astype(o_ref.dtype)

    n_items = len(items)
    scores(0)
    scores(1)
    softmax(0)
    for t in range(n_items):
        if t + 2 < n_items:
            scores(t + 2)
        if t + 1 < n_items:
            softmax(t + 1)
        finish(t)


def _attn_a(q_all, k_all, vt_all, g_all, bias, width_a):
    B, S, _ = q_all.shape
    step = A_STEP_BLOCKS * A_Q
    grid = (B, S // step)
    q_block = pl.BlockSpec((1, step, width_a), lambda b, i: (b, i, 0))
    return pl.pallas_call(
        _attn_a_kernel,
        grid=grid,
        in_specs=[
            q_block,
            pl.BlockSpec((1, S, width_a), lambda b, i: (b, 0, 0)),
            pl.BlockSpec((1, S // LANES, width_a, LANES), lambda b, i: (b, 0, 0, 0)),
            q_block,
            pl.BlockSpec(bias.shape, lambda b, i: (0, 0, 0)),
        ],
        out_specs=q_block,
        out_shape=jax.ShapeDtypeStruct((B, S, width_a), jnp.bfloat16),
        scratch_shapes=[
            pltpu.VMEM((2, A_WIN, 2 * A_Q), jnp.float32),
            pltpu.VMEM((2, A_WIN, 2 * A_Q), jnp.bfloat16),
        ],
        compiler_params=pltpu.CompilerParams(
            dimension_semantics=("arbitrary", "arbitrary"), vmem_limit_bytes=VMEM_LIMIT_BYTES),
    )(q_all, k_all, vt_all, g_all, bias)


def _bias_table(rel_bias):
    n_heads = rel_bias.shape[0]
    n_blk = A_BIAS_ROWS // A_Q
    e = np.arange(2 * A_Q)
    d = np.where(e < A_Q, e, e - 2 * A_Q)
    a = np.arange(n_blk)[:, None]
    idx = np.clip(A_LEFT_BLOCKS * A_Q - A_Q * a + d[None, :], -REL_CLIP, REL_CLIP) + REL_CLIP
    f = rel_bias.astype(jnp.float32)[:, idx]
    t = jnp.tile(f, (1, 1, A_Q))[..., :A_Q * (2 * A_Q - 1)]
    t = t.reshape(n_heads, n_blk, A_Q, 2 * A_Q - 1)[..., :A_Q]
    t = t.reshape(n_heads, A_BIAS_ROWS, A_Q)
    kk = np.arange(A_BIAS_ROWS)[:, None]
    r = np.arange(A_Q)[None, :]
    dchunk = r // CHUNK - kk // CHUNK + N_LEFT_CHUNKS
    valid = (dchunk >= 0) & (dchunk <= N_LEFT_CHUNKS)
    t = jnp.where(valid[None], t * LOG2E, NEG_INF)
    t = t.reshape(n_heads // 2, 2, A_BIAS_ROWS, A_Q).transpose(0, 2, 1, 3)
    return t.reshape(n_heads // 2, A_BIAS_ROWS, 2 * A_Q)


def _attn_b_kernel(q_ref, k_ref, vt_ref, g_ref, lq1_ref, lk1_ref, lq2_ref, lk2_ref, sub_ref,
                   o_ref, qm_ref, s_ref, m_ref, acc_ref, *, lam_init):
    i = pl.program_id(2)
    lane = lax.broadcasted_iota(jnp.int32, (B_Q, LANES), 1)
    first = lane < HEAD_DIM
    qp = q_ref[0]
    zero = jnp.zeros_like(qp)
    qm_ref[0] = jnp.where(first, qp, zero)
    qm_ref[1] = jnp.where(first, zero, qp)
    acc_ref[...] = jnp.zeros_like(acc_ref)
    m_ref[...] = jnp.full(m_ref.shape, NEG_INF, jnp.float32)
    ones = jnp.ones((BF16_SUBLANES, B_K), jnp.bfloat16)
    blocks_per_step = B_K // LANES

    def scores(j, slot):
        kj = k_ref[0, pl.ds(pl.multiple_of(j * B_K, B_K), B_K), :]
        for n in range(2):
            s_ref[slot, n] = lax.dot_general(kj, qm_ref[n], _NT,
                                             preferred_element_type=jnp.float32)

    def accumulate(j, slot, masked=False):
        vj = jnp.concatenate([vt_ref[0, j * blocks_per_step + t] for t in range(blocks_per_step)],
                             axis=1)
        vj = jnp.concatenate([vj, ones], axis=0)
        for n in range(2):
            s = s_ref[slot, n]
            if masked:
                kc = lax.broadcasted_iota(jnp.int32, (B_K, B_Q), 0) // CHUNK
                qc = lax.broadcasted_iota(jnp.int32, (B_K, B_Q), 1) // CHUNK
                s = jnp.where(kc <= qc, s, NEG_INF)
            m_old = m_ref[n]
            m_new = jnp.maximum(m_old, jnp.max(s, axis=0, keepdims=True))
            alpha = jnp.exp2(m_old - m_new)
            p = jnp.exp2(s - m_new).astype(jnp.bfloat16)
            acc_ref[n] = alpha * acc_ref[n] + jnp.dot(vj, p, preferred_element_type=jnp.float32)
            m_ref[n] = m_new

    def two_blocks(jj, carry):
        j = 2 * jj
        scores(j + 1, 1)
        accumulate(j, 0)
        scores(j + 2, 0)
        accumulate(j + 1, 1)
        return carry

    scores(0, 0)
    lax.fori_loop(0, i // 2, two_blocks, 0)

    @pl.when(i % 2 == 1)
    def _():
        scores(i, 1)
        accumulate(i - 1, 0)
        accumulate(i, 1, masked=True)

    @pl.when(i % 2 == 0)
    def _():
        accumulate(i, 0, masked=True)

    lam = (jnp.exp(jnp.sum(lq1_ref[...] * lk1_ref[...], axis=1, keepdims=True))
           - jnp.exp(jnp.sum(lq2_ref[...] * lk2_ref[...], axis=1, keepdims=True)) + lam_init)
    l0 = acc_ref[0, LANES:LANES + 1, :]
    l1 = acc_ref[1, LANES:LANES + 1, :]
    ot = acc_ref[0, :LANES, :] * (1.0 / l0) - lam * (acc_ref[1, :LANES, :] * (1.0 / l1))
    ot = ot * lax.rsqrt(jnp.mean(ot * ot, axis=0, keepdims=True) + RMS_EPS)
    o = ot.T * (sub_ref[...] * (1.0 - lam_init))
    g = g_ref[0].astype(jnp.float32)
    o_ref[0] = (o * _silu(g)).astype(o_ref.dtype)


def _attn_b(q_all, k_all, vt_all, g_all, lq1, lk1, lq2, lk2, subln, width_a, lam_init):
    B, S, width = q_all.shape
    n_heads = (width - width_a) // LANES
    col0 = width_a // LANES
    grid = (B, n_heads, S // B_Q)
    q_block = pl.BlockSpec((1, B_Q, LANES), lambda b, h, i: (b, i, col0 + h))
    vec64 = pl.BlockSpec((1, HEAD_DIM), lambda b, h, i: (0, 0))
    return pl.pallas_call(
        functools.partial(_attn_b_kernel, lam_init=lam_init),
        grid=grid,
        in_specs=[
            q_block,
            pl.BlockSpec((1, S, LANES), lambda b, h, i: (b, 0, col0 + h)),
            pl.BlockSpec((1, S // LANES, LANES, LANES), lambda b, h, i: (b, 0, col0 + h, 0)),
            q_block,
            vec64, vec64, vec64, vec64,
            pl.BlockSpec((1, LANES), lambda b, h, i: (0, 0)),
        ],
        out_specs=pl.BlockSpec((1, B_Q, LANES), lambda b, h, i: (b, i, h)),
        out_shape=jax.ShapeDtypeStruct((B, S, width - width_a), jnp.bfloat16),
        scratch_shapes=[
            pltpu.VMEM((2, B_Q, LANES), jnp.bfloat16),
            pltpu.VMEM((2, 2, B_K, B_Q), jnp.float32),
            pltpu.VMEM((2, 1, B_Q), jnp.float32),
            pltpu.VMEM((2, ACC_ROWS, B_Q), jnp.float32),
        ],
        compiler_params=pltpu.CompilerParams(
            dimension_semantics=("arbitrary", "arbitrary", "arbitrary"),
            vmem_limit_bytes=VMEM_LIMIT_BYTES),
    )(q_all, k_all, vt_all, g_all, lq1, lk1, lq2, lk2, subln)


def _out_proj_kernel(x_ref, ya_ref, yb_ref, wa_ref, wb_ref, g_ref, b_ref, o_ref, *, alpha):
    y = jnp.dot(ya_ref[0], wa_ref[...], preferred_element_type=jnp.float32)
    y = y + jnp.dot(yb_ref[0], wb_ref[...], preferred_element_type=jnp.float32)
    z = alpha * x_ref[0] + y
    mu = jnp.mean(z, axis=1, keepdims=True)
    zc = z - mu
    var = jnp.mean(zc * zc, axis=1, keepdims=True)
    o_ref[0] = zc * lax.rsqrt(var + LN_EPS) * g_ref[...] + b_ref[...]


def _out_proj(x, ya, yb, wa, wb, ln_g, ln_b, alpha):
    B, S, D = x.shape
    grid = (B, S // PROJ_ROWS)
    x_block = pl.BlockSpec((1, PROJ_ROWS, D), lambda b, i: (b, i, 0))
    vec = pl.BlockSpec((1, D), lambda b, i: (0, 0))
    return pl.pallas_call(
        functools.partial(_out_proj_kernel, alpha=alpha),
        grid=grid,
        in_specs=[
            x_block,
            pl.BlockSpec((1, PROJ_ROWS, ya.shape[2]), lambda b, i: (b, i, 0)),
            pl.BlockSpec((1, PROJ_ROWS, yb.shape[2]), lambda b, i: (b, i, 0)),
            pl.BlockSpec(wa.shape, lambda b, i: (0, 0)),
            pl.BlockSpec(wb.shape, lambda b, i: (0, 0)),
            vec, vec,
        ],
        out_specs=x_block,
        out_shape=jax.ShapeDtypeStruct((B, S, D), jnp.float32),
        compiler_params=pltpu.CompilerParams(
            dimension_semantics=("arbitrary", "arbitrary"), vmem_limit_bytes=VMEM_LIMIT_BYTES),
    )(x, ya, yb, wa, wb, ln_g, ln_b)


def _rope_tables(seq):
    half = HEAD_DIM // 2
    inv_freq = ROPE_THETA ** (-jnp.arange(0, HEAD_DIM, 2, dtype=jnp.float32) / HEAD_DIM)
    pos = jnp.arange(seq, dtype=jnp.float32)
    ang = pos[:, None] * inv_freq[None, :]
    ang = jnp.concatenate([ang] * (LANES // half), axis=-1)
    low = (np.arange(LANES) % HEAD_DIM) < half
    sin = jnp.sin(ang)
    return jnp.cos(ang), jnp.where(low, -sin, 0.0), jnp.where(low, 0.0, sin)


def kernel(x, w_in, w_out, rel_bias, lambda_q1, lambda_k1, lambda_q2, lambda_k2, subln_g, ln_g, ln_b):
    depth, d_model, proj_cols = w_in.shape
    seq = x.shape[1]
    width_a = proj_cols // 8
    alpha = (2 * depth) ** 0.25
    scale = HEAD_DIM ** -0.5
    cos, sina, sinb = _rope_tables(seq)
    for l in range(depth):
        w = w_in[l]
        cols = [w[:, j * width_a:(j + 1) * width_a] for j in range(8)]
        qa, ka, va, ga, qb, kb, vb, gb = cols
        w_std = jnp.concatenate([qa * (scale * LOG2E), qb * (scale * LOG2E), ka, kb, ga, gb], axis=1).astype(jnp.bfloat16)
        w_vt = jnp.concatenate([va, vb], axis=1).T.astype(jnp.bfloat16)
        q_all, k_all, g_all, vt_all = _in_proj(x, w_std, w_vt, cos, sina, sinb, width_a)

        ya = _attn_a(q_all, k_all, vt_all, g_all, _bias_table(rel_bias[l]), width_a)
        lam_init = 0.8 - 0.6 * math.exp(-0.3 * l)
        yb = _attn_b(q_all, k_all, vt_all, g_all,
                     lambda_q1[l][None].astype(jnp.float32), lambda_k1[l][None].astype(jnp.float32),
                     lambda_q2[l][None].astype(jnp.float32), lambda_k2[l][None].astype(jnp.float32),
                     subln_g[l][None].astype(jnp.float32), width_a, lam_init)

        wo = w_out[l].astype(jnp.bfloat16)
        x = _out_proj(x, ya, yb, wo[:width_a], wo[width_a:], ln_g[l][None], ln_b[l][None], alpha)
    return x
```

```python
import functools
import math

import numpy as np
import jax
import jax.numpy as jnp
from jax import lax
from jax.experimental import pallas as pl
from jax.experimental.pallas import tpu as pltpu

CHUNK = 64
N_LEFT_CHUNKS = 8
HEAD_DIM = 64
REL_CLIP = 128
ROPE_THETA = 10000.0
LN_EPS = 1e-5
RMS_EPS = 1e-5
NEG_INF = -1e30

LANES = 128
VMEM_LIMIT_BYTES = 52 * 1024 * 1024

PROJ_ROWS = 512
PROJ_COLS_CHUNK = 512
A_Q = 128
A_WIN = A_Q + N_LEFT_CHUNKS * CHUNK
A_LEFT_BLOCKS = N_LEFT_CHUNKS * CHUNK // A_Q
A_BIAS_ROWS = A_WIN + A_LEFT_BLOCKS * A_Q
A_STEP_BLOCKS = 8
B_Q = 512
B_K = 512
BF16_SUBLANES = 16
ACC_ROWS = LANES + BF16_SUBLANES
LOG2E = 1.4426950408889634

_NT = (((1,), (1,)), ((), ()))


def _silu(g):
    return g * (1.0 / (1.0 + jnp.exp(-g)))


def _in_proj_kernel(x_ref, w_ref, wvt_ref, cos_ref, sina_ref, sinb_ref,
                    q_ref, k_ref, g_ref, vt_ref, *, width_a):
    xb = x_ref[0].astype(jnp.bfloat16)
    n_std = w_ref.shape[1] // PROJ_COLS_CHUNK
    per_kind = n_std // 3
    outs = (q_ref, k_ref, g_ref)
    cos = cos_ref[...]
    sina = sina_ref[...]
    sinb = sinb_ref[...]
    for c in range(n_std):
        kind, sub = divmod(c, per_kind)
        col0 = sub * PROJ_COLS_CHUNK
        r = jnp.dot(xb, w_ref[:, c * PROJ_COLS_CHUNK:(c + 1) * PROJ_COLS_CHUNK],
                    preferred_element_type=jnp.float32)
        rotary = kind < 2 and col0 >= width_a
        for t in range(PROJ_COLS_CHUNK // LANES):
            rt = r[:, t * LANES:(t + 1) * LANES]
            if rotary:
                rt = (rt * cos + pltpu.roll(rt, LANES - HEAD_DIM // 2, 1) * sina
                      + pltpu.roll(rt, HEAD_DIM // 2, 1) * sinb)
            outs[kind][0, :, col0 + t * LANES:col0 + (t + 1) * LANES] = rt.astype(jnp.bfloat16)
    rows = xb.shape[0]
    n_v = wvt_ref.shape[0] // PROJ_COLS_CHUNK
    for c in range(n_v):
        rt = lax.dot_general(wvt_ref[c * PROJ_COLS_CHUNK:(c + 1) * PROJ_COLS_CHUNK, :], xb, _NT,
                             preferred_element_type=jnp.float32)
        for j in range(rows // LANES):
            vt_ref[0, j, c * PROJ_COLS_CHUNK:(c + 1) * PROJ_COLS_CHUNK, :] = (
                rt[:, j * LANES:(j + 1) * LANES].astype(jnp.bfloat16))


def _in_proj(x, w_std, w_vt, cos, sina, sinb, width_a):
    B, S, D = x.shape
    n3 = w_std.shape[1]
    width = n3 // 3
    grid = (B, S // PROJ_ROWS)
    row_block = pl.BlockSpec((1, PROJ_ROWS, width), lambda b, i: (b, i, 0))
    tab_block = pl.BlockSpec((PROJ_ROWS, LANES), lambda b, i: (i, 0))
    return pl.pallas_call(
        functools.partial(_in_proj_kernel, width_a=width_a),
        grid=grid,
        in_specs=[
            pl.BlockSpec((1, PROJ_ROWS, D), lambda b, i: (b, i, 0)),
            pl.BlockSpec((D, n3), lambda b, i: (0, 0)),
            pl.BlockSpec((width, D), lambda b, i: (0, 0)),
            tab_block, tab_block, tab_block,
        ],
        out_specs=[
            row_block, row_block, row_block,
            pl.BlockSpec((1, PROJ_ROWS // LANES, width, LANES), lambda b, i: (b, i, 0, 0)),
        ],
        out_shape=[
            jax.ShapeDtypeStruct((B, S, width), jnp.bfloat16),
            jax.ShapeDtypeStruct((B, S, width), jnp.bfloat16),
            jax.ShapeDtypeStruct((B, S, width), jnp.bfloat16),
            jax.ShapeDtypeStruct((B, S // LANES, width, LANES), jnp.bfloat16),
        ],
        compiler_params=pltpu.CompilerParams(
            dimension_semantics=("arbitrary", "arbitrary"), vmem_limit_bytes=VMEM_LIMIT_BYTES),
    )(x, w_std, w_vt, cos, sina, sinb)


def _attn_a_kernel(q_ref, k_ref, vt_ref, g_ref, bias_ref, o_ref, s_ref, p_ref):
    i = pl.program_id(1)
    n_pairs = q_ref.shape[2] // LANES
    lane = lax.broadcasted_iota(jnp.int32, (A_Q, LANES), 1)
    first = lane < HEAD_DIM
    ones = jnp.ones((BF16_SUBLANES, A_WIN), jnp.bfloat16)
    items = [(sb, p) for sb in range(A_STEP_BLOCKS) for p in range(n_pairs)]

    def first_key_block(sb):
        return jnp.maximum(i * A_STEP_BLOCKS + sb - A_LEFT_BLOCKS, 0)

    def scores(t):
        sb, p = items[t]
        cols = slice(p * LANES, (p + 1) * LANES)
        qp = q_ref[0, sb * A_Q:(sb + 1) * A_Q, cols]
        zero = jnp.zeros_like(qp)
        q2 = jnp.concatenate([jnp.where(first, qp, zero), jnp.where(first, zero, qp)], axis=0)
        start = pl.multiple_of(first_key_block(sb) * A_Q, A_Q)
        kw = k_ref[0, pl.ds(start, A_WIN), cols]
        s_ref[t % 2] = lax.dot_general(kw, q2, _NT, preferred_element_type=jnp.float32)

    def softmax(t):
        sb, p = items[t]
        blk0 = first_key_block(sb)
        bias_row = pl.multiple_of((A_LEFT_BLOCKS - (i * A_STEP_BLOCKS + sb - blk0)) * A_Q, A_Q)
        s = s_ref[t % 2] + bias_ref[p, pl.ds(bias_row, A_WIN), :]
        m = jnp.max(s, axis=0, keepdims=True)
        p_ref[t % 2] = jnp.exp2(s - m).astype(jnp.bfloat16)

    def finish(t):
        sb, p = items[t]
        cols = slice(p * LANES, (p + 1) * LANES)
        rows = slice(sb * A_Q, (sb + 1) * A_Q)
        blk0 = first_key_block(sb)
        vw = jnp.concatenate([vt_ref[0, blk0 + u, cols, :] for u in range(A_WIN // LANES)], axis=1)
        vw = jnp.concatenate([vw, ones], axis=0)
        r = jnp.dot(vw, p_ref[t % 2], preferred_element_type=jnp.float32)
        r = r[:LANES] * (1.0 / r[LANES:LANES + 1])
        ot = jnp.concatenate([r[:HEAD_DIM, :A_Q], r[HEAD_DIM:, A_Q:]], axis=0)
        g = g_ref[0, rows, cols].astype(jnp.float32)
        o_ref[0, rows, cols] = (ot.T * _silu(g)).astype(o_ref.dtype)

    n_items = len(items)
    scores(0)
    scores(1)
    softmax(0)
    for t in range(n_items):
        if t + 2 < n_items:
            scores(t + 2)
        if t + 1 < n_items:
            softmax(t + 1)
        finish(t)


def _attn_a(q_all, k_all, vt_all, g_all, bias, width_a):
    B, S, _ = q_all.shape
    step = A_STEP_BLOCKS * A_Q
    grid = (B, S // step)
    q_block = pl.BlockSpec((1, step, width_a), lambda b, i: (b, i, 0))
    return pl.pallas_call(
        _attn_a_kernel,
        grid=grid,
        in_specs=[
            q_block,
            pl.BlockSpec((1, S, width_a), lambda b, i: (b, 0, 0)),
            pl.BlockSpec((1, S // LANES, width_a, LANES), lambda b, i: (b, 0, 0, 0)),
            q_block,
            pl.BlockSpec(bias.shape, lambda b, i: (0, 0, 0)),
        ],
        out_specs=q_block,
        out_shape=jax.ShapeDtypeStruct((B, S, width_a), jnp.bfloat16),
        scratch_shapes=[
            pltpu.VMEM((2, A_WIN, 2 * A_Q), jnp.float32),
            pltpu.VMEM((2, A_WIN, 2 * A_Q), jnp.bfloat16),
        ],
        compiler_params=pltpu.CompilerParams(
            dimension_semantics=("arbitrary", "arbitrary"), vmem_limit_bytes=VMEM_LIMIT_BYTES),
    )(q_all, k_all, vt_all, g_all, bias)


def _bias_table(rel_bias):
    n_heads = rel_bias.shape[0]
    n_blk = A_BIAS_ROWS // A_Q
    e = np.arange(2 * A_Q)
    d = np.where(e < A_Q, e, e - 2 * A_Q)
    a = np.arange(n_blk)[:, None]
    idx = np.clip(A_LEFT_BLOCKS * A_Q - A_Q * a + d[None, :], -REL_CLIP, REL_CLIP) + REL_CLIP
    f = rel_bias.astype(jnp.float32)[:, idx]
    t = jnp.tile(f, (1, 1, A_Q))[..., :A_Q * (2 * A_Q - 1)]
    t = t.reshape(n_heads, n_blk, A_Q, 2 * A_Q - 1)[..., :A_Q]
    t = t.reshape(n_heads, A_BIAS_ROWS, A_Q)
    kk = np.arange(A_BIAS_ROWS)[:, None]
    r = np.arange(A_Q)[None, :]
    dchunk = r // CHUNK - kk // CHUNK + N_LEFT_CHUNKS
    valid = (dchunk >= 0) & (dchunk <= N_LEFT_CHUNKS)
    t = jnp.where(valid[None], t * LOG2E, NEG_INF)
    t = t.reshape(n_heads // 2, 2, A_BIAS_ROWS, A_Q).transpose(0, 2, 1, 3)
    return t.reshape(n_heads // 2, A_BIAS_ROWS, 2 * A_Q)


def _attn_b_kernel(q_ref, qn_ref, k_ref, vt_ref, g_ref, lq1_ref, lk1_ref, lq2_ref, lk2_ref, sub_ref,
                   o_ref, qm_ref, s_ref, m_ref, acc_ref, *, lam_init):
    i = pl.program_id(2)
    lane = lax.broadcasted_iota(jnp.int32, (B_Q, LANES), 1)
    first = lane < HEAD_DIM
    acc_ref[...] = jnp.zeros_like(acc_ref)
    m_ref[...] = jnp.full(m_ref.shape, NEG_INF, jnp.float32)
    ones = jnp.ones((BF16_SUBLANES, B_K), jnp.bfloat16)
    blocks_per_step = B_K // LANES

    def scores(j, slot):
        kj = k_ref[0, pl.ds(pl.multiple_of(j * B_K, B_K), B_K), :]
        for n in range(2):
            s_ref[slot, n] = lax.dot_general(kj, qm_ref[n], _NT,
                                             preferred_element_type=jnp.float32)

    def accumulate(j, slot, masked=False):
        vj = jnp.concatenate([vt_ref[0, j * blocks_per_step + t] for t in range(blocks_per_step)],
                             axis=1)
        vj = jnp.concatenate([vj, ones], axis=0)
        for n in range(2):
            s = s_ref[slot, n]
            if masked:
                kc = lax.broadcasted_iota(jnp.int32, (B_K, B_Q), 0) // CHUNK
                qc = lax.broadcasted_iota(jnp.int32, (B_K, B_Q), 1) // CHUNK
                s = jnp.where(kc <= qc, s, NEG_INF)
            m_old = m_ref[n]
            m_new = jnp.maximum(m_old, jnp.max(s, axis=0, keepdims=True))
            alpha = jnp.exp2(m_old - m_new)
            p = jnp.exp2(s - m_new).astype(jnp.bfloat16)
            acc_ref[n] = alpha * acc_ref[n] + jnp.dot(vj, p, preferred_element_type=jnp.float32)
            m_ref[n] = m_new

    def two_blocks(jj, carry):
        j = 2 * jj
        scores(j + 1, 1)
        accumulate(j, 0)
        scores(j + 2, 0)
        accumulate(j + 1, 1)
        return carry

    def first_scores(ref):
        qp = ref[0]
        zero = jnp.zeros_like(qp)
        qm_ref[0] = jnp.where(first, qp, zero)
        qm_ref[1] = jnp.where(first, zero, qp)
        scores(0, 0)

    @pl.when(i == 0)
    def _():
        first_scores(q_ref)

    lax.fori_loop(0, i // 2, two_blocks, 0)

    @pl.when(i % 2 == 1)
    def _():
        scores(i, 1)
        accumulate(i - 1, 0)

    accumulate(i, i % 2, masked=True)
    first_scores(qn_ref)

    lam = (jnp.exp(jnp.sum(lq1_ref[...] * lk1_ref[...], axis=1, keepdims=True))
           - jnp.exp(jnp.sum(lq2_ref[...] * lk2_ref[...], axis=1, keepdims=True)) + lam_init)
    l0 = acc_ref[0, LANES:LANES + 1, :]
    l1 = acc_ref[1, LANES:LANES + 1, :]
    ot = acc_ref[0, :LANES, :] * (1.0 / l0) - lam * (acc_ref[1, :LANES, :] * (1.0 / l1))
    ot = ot * lax.rsqrt(jnp.mean(ot * ot, axis=0, keepdims=True) + RMS_EPS)
    o = ot.T * (sub_ref[...] * (1.0 - lam_init))
    g = g_ref[0].astype(jnp.float32)
    o_ref[0] = (o * _silu(g)).astype(o_ref.dtype)


def _attn_b(q_all, k_all, vt_all, g_all, lq1, lk1, lq2, lk2, subln, width_a, lam_init):
    B, S, width = q_all.shape
    n_heads = (width - width_a) // LANES
    col0 = width_a // LANES
    n_q = S // B_Q
    grid = (B, n_heads, n_q)
    q_block = pl.BlockSpec((1, B_Q, LANES), lambda b, h, i: (b, i, col0 + h))
    vec64 = pl.BlockSpec((1, HEAD_DIM), lambda b, h, i: (0, 0))
    return pl.pallas_call(
        functools.partial(_attn_b_kernel, lam_init=lam_init),
        grid=grid,
        in_specs=[
            q_block,
            pl.BlockSpec((1, B_Q, LANES), lambda b, h, i: (b, jnp.minimum(i + 1, n_q - 1), col0 + h)),
            pl.BlockSpec((1, S, LANES), lambda b, h, i: (b, 0, col0 + h)),
            pl.BlockSpec((1, S // LANES, LANES, LANES), lambda b, h, i: (b, 0, col0 + h, 0)),
            q_block,
            vec64, vec64, vec64, vec64,
            pl.BlockSpec((1, LANES), lambda b, h, i: (0, 0)),
        ],
        out_specs=pl.BlockSpec((1, B_Q, LANES), lambda b, h, i: (b, i, h)),
        out_shape=jax.ShapeDtypeStruct((B, S, width - width_a), jnp.bfloat16),
        scratch_shapes=[
            pltpu.VMEM((2, B_Q, LANES), jnp.bfloat16),
            pltpu.VMEM((2, 2, B_K, B_Q), jnp.float32),
            pltpu.VMEM((2, 1, B_Q), jnp.float32),
            pltpu.VMEM((2, ACC_ROWS, B_Q), jnp.float32),
        ],
        compiler_params=pltpu.CompilerParams(
            dimension_semantics=("arbitrary", "arbitrary", "arbitrary"),
            vmem_limit_bytes=VMEM_LIMIT_BYTES),
    )(q_all, q_all, k_all, vt_all, g_all, lq1, lk1, lq2, lk2, subln)


def _out_proj_kernel(x_ref, ya_ref, yb_ref, wa_ref, wb_ref, g_ref, b_ref, o_ref, *, alpha):
    y = jnp.dot(ya_ref[0], wa_ref[...], preferred_element_type=jnp.float32)
    y = y + jnp.dot(yb_ref[0], wb_ref[...], preferred_element_type=jnp.float32)
    z = alpha * x_ref[0] + y
    mu = jnp.mean(z, axis=1, keepdims=True)
    zc = z - mu
    var = jnp.mean(zc * zc, axis=1, keepdims=True)
    o_ref[0] = zc * lax.rsqrt(var + LN_EPS) * g_ref[...] + b_ref[...]


def _out_proj(x, ya, yb, wa, wb, ln_g, ln_b, alpha):
    B, S, D = x.shape
    grid = (B, S // PROJ_ROWS)
    x_block = pl.BlockSpec((1, PROJ_ROWS, D), lambda b, i: (b, i, 0))
    vec = pl.BlockSpec((1, D), lambda b, i: (0, 0))
    return pl.pallas_call(
        functools.partial(_out_proj_kernel, alpha=alpha),
        grid=grid,
        in_specs=[
            x_block,
            pl.BlockSpec((1, PROJ_ROWS, ya.shape[2]), lambda b, i: (b, i, 0)),
            pl.BlockSpec((1, PROJ_ROWS, yb.shape[2]), lambda b, i: (b, i, 0)),
            pl.BlockSpec(wa.shape, lambda b, i: (0, 0)),
            pl.BlockSpec(wb.shape, lambda b, i: (0, 0)),
            vec, vec,
        ],
        out_specs=x_block,
        out_shape=jax.ShapeDtypeStruct((B, S, D), jnp.float32),
        compiler_params=pltpu.CompilerParams(
            dimension_semantics=("arbitrary", "arbitrary"), vmem_limit_bytes=VMEM_LIMIT_BYTES),
    )(x, ya, yb, wa, wb, ln_g, ln_b)


def _rope_tables(seq):
    half = HEAD_DIM // 2
    inv_freq = ROPE_THETA ** (-jnp.arange(0, HEAD_DIM, 2, dtype=jnp.float32) / HEAD_DIM)
    pos = jnp.arange(seq, dtype=jnp.float32)
    ang = pos[:, None] * inv_freq[None, :]
    ang = jnp.concatenate([ang] * (LANES // half), axis=-1)
    low = (np.arange(LANES) % HEAD_DIM) < half
    sin = jnp.sin(ang)
    return jnp.cos(ang), jnp.where(low, -sin, 0.0), jnp.where(low, 0.0, sin)


def kernel(x, w_in, w_out, rel_bias, lambda_q1, lambda_k1, lambda_q2, lambda_k2, subln_g, ln_g, ln_b):
    depth, d_model, proj_cols = w_in.shape
    seq = x.shape[1]
    width_a = proj_cols // 8
    alpha = (2 * depth) ** 0.25
    scale = HEAD_DIM ** -0.5
    cos, sina, sinb = _rope_tables(seq)
    for l in range(depth):
        w = w_in[l]
        cols = [w[:, j * width_a:(j + 1) * width_a] for j in range(8)]
        qa, ka, va, ga, qb, kb, vb, gb = cols
        w_std = jnp.concatenate([qa * (scale * LOG2E), qb * (scale * LOG2E), ka, kb, ga, gb], axis=1).astype(jnp.bfloat16)
        w_vt = jnp.concatenate([va, vb], axis=1).T.astype(jnp.bfloat16)
        q_all, k_all, g_all, vt_all = _in_proj(x, w_std, w_vt, cos, sina, sinb, width_a)

        ya = _attn_a(q_all, k_all, vt_all, g_all, _bias_table(rel_bias[l]), width_a)
        lam_init = 0.8 - 0.6 * math.exp(-0.3 * l)
        yb = _attn_b(q_all, k_all, vt_all, g_all,
                     lambda_q1[l][None].astype(jnp.float32), lambda_k1[l][None].astype(jnp.float32),
                     lambda_q2[l][None].astype(jnp.float32), lambda_k2[l][None].astype(jnp.float32),
                     subln_g[l][None].astype(jnp.float32), width_a, lam_init)

        wo = w_out[l].astype(jnp.bfloat16)
        x = _out_proj(x, ya, yb, wo[:width_a], wo[width_a:], ln_g[l][None], ln_b[l][None], alpha)
    return x
```

```python
import functools
import math

import numpy as np
import jax
import jax.numpy as jnp
from jax import lax
from jax.experimental import pallas as pl
from jax.experimental.pallas import tpu as pltpu

CHUNK = 64
N_LEFT_CHUNKS = 8
HEAD_DIM = 64
REL_CLIP = 128
ROPE_THETA = 10000.0
LN_EPS = 1e-5
RMS_EPS = 1e-5
NEG_INF = -1e30

LANES = 128
VMEM_LIMIT_BYTES = 52 * 1024 * 1024

PROJ_ROWS = 512
PROJ_COLS_CHUNK = 512
A_Q = 128
A_WIN = A_Q + N_LEFT_CHUNKS * CHUNK
A_LEFT_BLOCKS = N_LEFT_CHUNKS * CHUNK // A_Q
A_BIAS_ROWS = A_WIN + A_LEFT_BLOCKS * A_Q
A_STEP_BLOCKS = 8
B_Q = 512
B_K = 512
BF16_SUBLANES = 16
ACC_ROWS = LANES + BF16_SUBLANES
LOG2E = 1.4426950408889634

_NT = (((1,), (1,)), ((), ()))


def _silu(g):
    return g * (1.0 / (1.0 + jnp.exp(-g)))


def _in_proj_kernel(x_ref, w_ref, wvt_ref, cos_ref, sina_ref, sinb_ref,
                    q_ref, k_ref, g_ref, vt_ref, *, width_a):
    xb = x_ref[0].astype(jnp.bfloat16)
    n_std = w_ref.shape[1] // PROJ_COLS_CHUNK
    per_kind = n_std // 3
    outs = (q_ref, k_ref, g_ref)
    cos = cos_ref[...]
    sina = sina_ref[...]
    sinb = sinb_ref[...]
    for c in range(n_std):
        kind, sub = divmod(c, per_kind)
        col0 = sub * PROJ_COLS_CHUNK
        r = jnp.dot(xb, w_ref[:, c * PROJ_COLS_CHUNK:(c + 1) * PROJ_COLS_CHUNK],
                    preferred_element_type=jnp.float32)
        rotary = kind < 2 and col0 >= width_a
        for t in range(PROJ_COLS_CHUNK // LANES):
            rt = r[:, t * LANES:(t + 1) * LANES]
            if rotary:
                rt = (rt * cos + pltpu.roll(rt, LANES - HEAD_DIM // 2, 1) * sina
                      + pltpu.roll(rt, HEAD_DIM // 2, 1) * sinb)
            outs[kind][0, :, col0 + t * LANES:col0 + (t + 1) * LANES] = rt.astype(jnp.bfloat16)
    rows = xb.shape[0]
    n_v = wvt_ref.shape[0] // PROJ_COLS_CHUNK
    for c in range(n_v):
        rt = lax.dot_general(wvt_ref[c * PROJ_COLS_CHUNK:(c + 1) * PROJ_COLS_CHUNK, :], xb, _NT,
                             preferred_element_type=jnp.float32)
        for j in range(rows // LANES):
            vt_ref[0, j, c * PROJ_COLS_CHUNK:(c + 1) * PROJ_COLS_CHUNK, :] = (
                rt[:, j * LANES:(j + 1) * LANES].astype(jnp.bfloat16))


def _in_proj(x, w_std, w_vt, cos, sina, sinb, width_a):
    B, S, D = x.shape
    n3 = w_std.shape[1]
    width = n3 // 3
    grid = (B, S // PROJ_ROWS)
    row_block = pl.BlockSpec((1, PROJ_ROWS, width), lambda b, i: (b, i, 0))
    tab_block = pl.BlockSpec((PROJ_ROWS, LANES), lambda b, i: (i, 0))
    return pl.pallas_call(
        functools.partial(_in_proj_kernel, width_a=width_a),
        grid=grid,
        in_specs=[
            pl.BlockSpec((1, PROJ_ROWS, D), lambda b, i: (b, i, 0)),
            pl.BlockSpec((D, n3), lambda b, i: (0, 0)),
            pl.BlockSpec((width, D), lambda b, i: (0, 0)),
            tab_block, tab_block, tab_block,
        ],
        out_specs=[
            row_block, row_block, row_block,
            pl.BlockSpec((1, PROJ_ROWS // LANES, width, LANES), lambda b, i: (b, i, 0, 0)),
        ],
        out_shape=[
            jax.ShapeDtypeStruct((B, S, width), jnp.bfloat16),
            jax.ShapeDtypeStruct((B, S, width), jnp.bfloat16),
            jax.ShapeDtypeStruct((B, S, width), jnp.bfloat16),
            jax.ShapeDtypeStruct((B, S // LANES, width, LANES), jnp.bfloat16),
        ],
        compiler_params=pltpu.CompilerParams(
            dimension_semantics=("arbitrary", "arbitrary"), vmem_limit_bytes=VMEM_LIMIT_BYTES),
    )(x, w_std, w_vt, cos, sina, sinb)


def _attn_a_kernel(q_ref, k_ref, vt_ref, g_ref, f_ref, o_ref, bias_ref, s_ref, p_ref):
    i = pl.program_id(1)
    n_pairs = q_ref.shape[2] // LANES

    @pl.when((pl.program_id(0) == 0) & (i == 0))
    def _():
        _fill_bias_table(f_ref, bias_ref)

    lane = lax.broadcasted_iota(jnp.int32, (A_Q, LANES), 1)
    first = lane < HEAD_DIM
    ones = jnp.ones((BF16_SUBLANES, A_WIN), jnp.bfloat16)
    items = [(sb, p) for sb in range(A_STEP_BLOCKS) for p in range(n_pairs)]

    def first_key_block(sb):
        return jnp.maximum(i * A_STEP_BLOCKS + sb - A_LEFT_BLOCKS, 0)

    def scores(t):
        sb, p = items[t]
        cols = slice(p * LANES, (p + 1) * LANES)
        qp = q_ref[0, sb * A_Q:(sb + 1) * A_Q, cols]
        zero = jnp.zeros_like(qp)
        q2 = jnp.concatenate([jnp.where(first, qp, zero), jnp.where(first, zero, qp)], axis=0)
        start = pl.multiple_of(first_key_block(sb) * A_Q, A_Q)
        kw = k_ref[0, pl.ds(start, A_WIN), cols]
        s_ref[t % 2] = lax.dot_general(kw, q2, _NT, preferred_element_type=jnp.float32)

    def softmax(t):
        sb, p = items[t]
        blk0 = first_key_block(sb)
        bias_row = pl.multiple_of((A_LEFT_BLOCKS - (i * A_STEP_BLOCKS + sb - blk0)) * A_Q, A_Q)
        s = s_ref[t % 2] + bias_ref[p, pl.ds(bias_row, A_WIN), :]
        m = jnp.max(s, axis=0, keepdims=True)
        p_ref[t % 2] = jnp.exp2(s - m).astype(jnp.bfloat16)

    def finish(t):
        sb, p = items[t]
        cols = slice(p * LANES, (p + 1) * LANES)
        rows = slice(sb * A_Q, (sb + 1) * A_Q)
        blk0 = first_key_block(sb)
        vw = jnp.concatenate([vt_ref[0, blk0 + u, cols, :] for u in range(A_WIN // LANES)], axis=1)
        vw = jnp.concatenate([vw, ones], axis=0)
        r = jnp.dot(vw, p_ref[t % 2], preferred_element_type=jnp.float32)
        r = r[:LANES] * (1.0 / r[LANES:LANES + 1])
        ot = jnp.concatenate([r[:HEAD_DIM, :A_Q], r[HEAD_DIM:, A_Q:]], axis=0)
        g = g_ref[0, rows, cols].astype(jnp.float32)
        o_ref[0, rows, cols] = (ot.T * _silu(g)).astype(o_ref.dtype)

    n_items = len(items)
    scores(0)
    scores(1)
    softmax(0)
    for t in range(n_items):
        if t + 2 < n_items:
            scores(t + 2)
        if t + 1 < n_items:
            softmax(t + 1)
        finish(t)


def _attn_a(q_all, k_all, vt_all, g_all, bias_rows, width_a):
    B, S, _ = q_all.shape
    step = A_STEP_BLOCKS * A_Q
    grid = (B, S // step)
    q_block = pl.BlockSpec((1, step, width_a), lambda b, i: (b, i, 0))
    return pl.pallas_call(
        _attn_a_kernel,
        grid=grid,
        in_specs=[
            q_block,
            pl.BlockSpec((1, S, width_a), lambda b, i: (b, 0, 0)),
            pl.BlockSpec((1, S // LANES, width_a, LANES), lambda b, i: (b, 0, 0, 0)),
            q_block,
            pl.BlockSpec(bias_rows.shape, lambda b, i: (0, 0, 0)),
        ],
        out_specs=q_block,
        out_shape=jax.ShapeDtypeStruct((B, S, width_a), jnp.bfloat16),
        scratch_shapes=[
            pltpu.VMEM((width_a // LANES, A_BIAS_ROWS, 2 * A_Q), jnp.float32),
            pltpu.VMEM((2, A_WIN, 2 * A_Q), jnp.float32),
            pltpu.VMEM((2, A_WIN, 2 * A_Q), jnp.bfloat16),
        ],
        compiler_params=pltpu.CompilerParams(
            dimension_semantics=("arbitrary", "arbitrary"), vmem_limit_bytes=VMEM_LIMIT_BYTES),
    )(q_all, k_all, vt_all, g_all, bias_rows)


def _bias_rows(rel_bias):
    n_blk = A_BIAS_ROWS // A_Q
    e = np.arange(2 * A_Q)
    d = np.where(e < A_Q, e, e - 2 * A_Q)
    a = np.arange(n_blk)[:, None]
    idx = np.clip(A_LEFT_BLOCKS * A_Q - A_Q * a + d[None, :], -REL_CLIP, REL_CLIP) + REL_CLIP
    return rel_bias.astype(jnp.float32)[:, idx] * LOG2E


def _fill_bias_table(f_ref, bias_ref):
    n_heads, n_blk, _ = f_ref.shape
    row = lax.broadcasted_iota(jnp.int32, (A_Q, A_Q), 0)
    qchunk = lax.broadcasted_iota(jnp.int32, (A_Q, A_Q), 1) // CHUNK
    for h in range(n_heads):
        pair, half = divmod(h, 2)
        for a in range(n_blk):
            f = jnp.broadcast_to(f_ref[h, a:a + 1, :], (A_Q, 2 * A_Q))
            t = pltpu.roll(f, 0, 1, stride=1, stride_axis=0)[:, :A_Q]
            dchunk = qchunk - (row + a * A_Q) // CHUNK + N_LEFT_CHUNKS
            valid = (dchunk >= 0) & (dchunk <= N_LEFT_CHUNKS)
            bias_ref[pair, a * A_Q:(a + 1) * A_Q, half * A_Q:(half + 1) * A_Q] = jnp.where(valid, t, NEG_INF)


def _attn_b_kernel(q_ref, k_ref, vt_ref, g_ref, lq1_ref, lk1_ref, lq2_ref, lk2_ref, sub_ref,
                   o_ref, qm_ref, s_ref, m_ref, acc_ref, *, lam_init):
    n_q = q_ref.shape[1] // B_Q
    lane = lax.broadcasted_iota(jnp.int32, (B_Q, LANES), 1)
    first = lane < HEAD_DIM
    ones = jnp.ones((BF16_SUBLANES, B_K), jnp.bfloat16)
    blocks_per_step = B_K // LANES
    lam = (jnp.exp(jnp.sum(lq1_ref[...] * lk1_ref[...], axis=1, keepdims=True))
           - jnp.exp(jnp.sum(lq2_ref[...] * lk2_ref[...], axis=1, keepdims=True)) + lam_init)

    def scores(j, slot):
        kj = k_ref[0, pl.ds(pl.multiple_of(j * B_K, B_K), B_K), :]
        for n in range(2):
            s_ref[slot, n] = lax.dot_general(kj, qm_ref[n], _NT,
                                             preferred_element_type=jnp.float32)

    def accumulate(j, slot, masked=False):
        vj = jnp.concatenate([vt_ref[0, j * blocks_per_step + t] for t in range(blocks_per_step)],
                             axis=1)
        vj = jnp.concatenate([vj, ones], axis=0)
        for n in range(2):
            s = s_ref[slot, n]
            if masked:
                kc = lax.broadcasted_iota(jnp.int32, (B_K, B_Q), 0) // CHUNK
                qc = lax.broadcasted_iota(jnp.int32, (B_K, B_Q), 1) // CHUNK
                s = jnp.where(kc <= qc, s, NEG_INF)
            m_old = m_ref[n]
            m_new = jnp.maximum(m_old, jnp.max(s, axis=0, keepdims=True))
            alpha = jnp.exp2(m_old - m_new)
            p = jnp.exp2(s - m_new).astype(jnp.bfloat16)
            acc_ref[n] = alpha * acc_ref[n] + jnp.dot(vj, p, preferred_element_type=jnp.float32)
            m_ref[n] = m_new

    def two_blocks(jj, carry):
        j = 2 * jj
        scores(j + 1, 1)
        accumulate(j, 0)
        scores(j + 2, 0)
        accumulate(j + 1, 1)
        return carry

    def first_scores(i):
        qp = q_ref[0, pl.ds(pl.multiple_of(i * B_Q, B_Q), B_Q), :]
        zero = jnp.zeros_like(qp)
        qm_ref[0] = jnp.where(first, qp, zero)
        qm_ref[1] = jnp.where(first, zero, qp)
        scores(0, 0)

    def query_block(i, carry):
        rows = pl.ds(pl.multiple_of(i * B_Q, B_Q), B_Q)
        acc_ref[...] = jnp.zeros_like(acc_ref)
        m_ref[...] = jnp.full(m_ref.shape, NEG_INF, jnp.float32)
        lax.fori_loop(0, i // 2, two_blocks, 0)

        @pl.when(i % 2 == 1)
        def _():
            scores(i, 1)
            accumulate(i - 1, 0)

        accumulate(i, i % 2, masked=True)
        first_scores(jnp.minimum(i + 1, n_q - 1))

        l0 = acc_ref[0, LANES:LANES + 1, :]
        l1 = acc_ref[1, LANES:LANES + 1, :]
        ot = acc_ref[0, :LANES, :] * (1.0 / l0) - lam * (acc_ref[1, :LANES, :] * (1.0 / l1))
        ot = ot * lax.rsqrt(jnp.mean(ot * ot, axis=0, keepdims=True) + RMS_EPS)
        o = ot.T * (sub_ref[...] * (1.0 - lam_init))
        g = g_ref[0, rows, :].astype(jnp.float32)
        o_ref[0, rows, :] = (o * _silu(g)).astype(o_ref.dtype)
        return carry

    first_scores(0)
    lax.fori_loop(0, n_q, query_block, 0)


def _attn_b(q_all, k_all, vt_all, g_all, lq1, lk1, lq2, lk2, subln, width_a, lam_init):
    B, S, width = q_all.shape
    n_heads = (width - width_a) // LANES
    col0 = width_a // LANES
    grid = (B, n_heads)
    seq_block = pl.BlockSpec((1, S, LANES), lambda b, h: (b, 0, col0 + h))
    vec64 = pl.BlockSpec((1, HEAD_DIM), lambda b, h: (0, 0))
    return pl.pallas_call(
        functools.partial(_attn_b_kernel, lam_init=lam_init),
        grid=grid,
        in_specs=[
            seq_block,
            seq_block,
            pl.BlockSpec((1, S // LANES, LANES, LANES), lambda b, h: (b, 0, col0 + h, 0)),
            seq_block,
            vec64, vec64, vec64, vec64,
            pl.BlockSpec((1, LANES), lambda b, h: (0, 0)),
        ],
        out_specs=pl.BlockSpec((1, S, LANES), lambda b, h: (b, 0, h)),
        out_shape=jax.ShapeDtypeStruct((B, S, width - width_a), jnp.bfloat16),
        scratch_shapes=[
            pltpu.VMEM((2, B_Q, LANES), jnp.bfloat16),
            pltpu.VMEM((2, 2, B_K, B_Q), jnp.float32),
            pltpu.VMEM((2, 1, B_Q), jnp.float32),
            pltpu.VMEM((2, ACC_ROWS, B_Q), jnp.float32),
        ],
        compiler_params=pltpu.CompilerParams(
            dimension_semantics=("arbitrary", "arbitrary"), vmem_limit_bytes=VMEM_LIMIT_BYTES),
    )(q_all, k_all, vt_all, g_all, lq1, lk1, lq2, lk2, subln)


def _out_proj_kernel(x_ref, ya_ref, yb_ref, wa_ref, wb_ref, g_ref, b_ref, o_ref, *, alpha):
    y = jnp.dot(ya_ref[0], wa_ref[...], preferred_element_type=jnp.float32)
    y = y + jnp.dot(yb_ref[0], wb_ref[...], preferred_element_type=jnp.float32)
    z = alpha * x_ref[0] + y
    mu = jnp.mean(z, axis=1, keepdims=True)
    zc = z - mu
    var = jnp.mean(zc * zc, axis=1, keepdims=True)
    o_ref[0] = zc * lax.rsqrt(var + LN_EPS) * g_ref[...] + b_ref[...]


def _out_proj(x, ya, yb, wa, wb, ln_g, ln_b, alpha):
    B, S, D = x.shape
    grid = (B, S // PROJ_ROWS)
    x_block = pl.BlockSpec((1, PROJ_ROWS, D), lambda b, i: (b, i, 0))
    vec = pl.BlockSpec((1, D), lambda b, i: (0, 0))
    return pl.pallas_call(
        functools.partial(_out_proj_kernel, alpha=alpha),
        grid=grid,
        in_specs=[
            x_block,
            pl.BlockSpec((1, PROJ_ROWS, ya.shape[2]), lambda b, i: (b, i, 0)),
            pl.BlockSpec((1, PROJ_ROWS, yb.shape[2]), lambda b, i: (b, i, 0)),
            pl.BlockSpec(wa.shape, lambda b, i: (0, 0)),
            pl.BlockSpec(wb.shape, lambda b, i: (0, 0)),
            vec, vec,
        ],
        out_specs=x_block,
        out_shape=jax.ShapeDtypeStruct((B, S, D), jnp.float32),
        compiler_params=pltpu.CompilerParams(
            dimension_semantics=("arbitrary", "arbitrary"), vmem_limit_bytes=VMEM_LIMIT_BYTES),
    )(x, ya, yb, wa, wb, ln_g, ln_b)


def _rope_tables(seq):
    half = HEAD_DIM // 2
    inv_freq = ROPE_THETA ** (-jnp.arange(0, HEAD_DIM, 2, dtype=jnp.float32) / HEAD_DIM)
    pos = jnp.arange(seq, dtype=jnp.float32)
    ang = pos[:, None] * inv_freq[None, :]
    ang = jnp.concatenate([ang] * (LANES // half), axis=-1)
    low = (np.arange(LANES) % HEAD_DIM) < half
    sin = jnp.sin(ang)
    return jnp.cos(ang), jnp.where(low, -sin, 0.0), jnp.where(low, 0.0, sin)


def kernel(x, w_in, w_out, rel_bias, lambda_q1, lambda_k1, lambda_q2, lambda_k2, subln_g, ln_g, ln_b):
    depth, d_model, proj_cols = w_in.shape
    seq = x.shape[1]
    width_a = proj_cols // 8
    alpha = (2 * depth) ** 0.25
    scale = HEAD_DIM ** -0.5
    cos, sina, sinb = _rope_tables(seq)
    for l in range(depth):
        w = w_in[l]
        cols = [w[:, j * width_a:(j + 1) * width_a] for j in range(8)]
        qa, ka, va, ga, qb, kb, vb, gb = cols
        w_std = jnp.concatenate([qa * (scale * LOG2E), qb * (scale * LOG2E), ka, kb, ga, gb], axis=1).astype(jnp.bfloat16)
        w_vt = jnp.concatenate([va, vb], axis=1).T.astype(jnp.bfloat16)
        q_all, k_all, g_all, vt_all = _in_proj(x, w_std, w_vt, cos, sina, sinb, width_a)

        ya = _attn_a(q_all, k_all, vt_all, g_all, _bias_rows(rel_bias[l]), width_a)
        lam_init = 0.8 - 0.6 * math.exp(-0.3 * l)
        yb = _attn_b(q_all, k_all, vt_all, g_all,
                     lambda_q1[l][None].astype(jnp.float32), lambda_k1[l][None].astype(jnp.float32),
                     lambda_q2[l][None].astype(jnp.float32), lambda_k2[l][None].astype(jnp.float32),
                     subln_g[l][None].astype(jnp.float32), width_a, lam_init)

        wo = w_out[l].astype(jnp.bfloat16)
        x = _out_proj(x, ya, yb, wo[:width_a], wo[width_a:], ln_g[l][None], ln_b[l][None], alpha)
    return x
```

```python
import functools
import math

import numpy as np
import jax
import jax.numpy as jnp
from jax import lax
from jax.experimental import pallas as pl
from jax.experimental.pallas import tpu as pltpu

CHUNK = 64
N_LEFT_CHUNKS = 8
HEAD_DIM = 64
REL_CLIP = 128
ROPE_THETA = 10000.0
LN_EPS = 1e-5
RMS_EPS = 1e-5
NEG_INF = -1e30

LANES = 128
VMEM_LIMIT_BYTES = 52 * 1024 * 1024

PROJ_ROWS = 512
PROJ_COLS_CHUNK = 512
OUT_SUB_ROWS = 256
A_Q = 128
A_WIN = A_Q + N_LEFT_CHUNKS * CHUNK
A_LEFT_BLOCKS = N_LEFT_CHUNKS * CHUNK // A_Q
A_BIAS_ROWS = A_WIN + A_LEFT_BLOCKS * A_Q
A_STEP_BLOCKS = 8
B_Q = 512
B_K = 512
BF16_SUBLANES = 16
ACC_ROWS = LANES + BF16_SUBLANES
LOG2E = 1.4426950408889634

_NT = (((1,), (1,)), ((), ()))


def _silu(g):
    return g * (1.0 / (1.0 + jnp.exp(-g)))


def _in_proj_kernel(x_ref, w_ref, wvt_ref, cos_ref, sina_ref, sinb_ref,
                    q_ref, k_ref, g_ref, vt_ref, *, width_a):
    xb = x_ref[0].astype(jnp.bfloat16)
    n_std = w_ref.shape[2] // PROJ_COLS_CHUNK
    per_kind = n_std // 3
    outs = (q_ref, k_ref, g_ref)
    cos = cos_ref[...]
    sina = sina_ref[...]
    sinb = sinb_ref[...]
    for c in range(n_std):
        kind, sub = divmod(c, per_kind)
        col0 = sub * PROJ_COLS_CHUNK
        r = jnp.dot(xb, w_ref[0, :, c * PROJ_COLS_CHUNK:(c + 1) * PROJ_COLS_CHUNK],
                    preferred_element_type=jnp.float32)
        rotary = kind < 2 and col0 >= width_a
        for t in range(PROJ_COLS_CHUNK // LANES):
            rt = r[:, t * LANES:(t + 1) * LANES]
            if rotary:
                rt = (rt * cos + pltpu.roll(rt, LANES - HEAD_DIM // 2, 1) * sina
                      + pltpu.roll(rt, HEAD_DIM // 2, 1) * sinb)
            outs[kind][0, :, col0 + t * LANES:col0 + (t + 1) * LANES] = rt.astype(jnp.bfloat16)
    rows = xb.shape[0]
    n_v = wvt_ref.shape[1] // PROJ_COLS_CHUNK
    for c in range(n_v):
        rt = lax.dot_general(wvt_ref[0, c * PROJ_COLS_CHUNK:(c + 1) * PROJ_COLS_CHUNK, :], xb, _NT,
                             preferred_element_type=jnp.float32)
        for j in range(rows // LANES):
            vt_ref[0, j, c * PROJ_COLS_CHUNK:(c + 1) * PROJ_COLS_CHUNK, :] = (
                rt[:, j * LANES:(j + 1) * LANES].astype(jnp.bfloat16))


def _in_proj(x, w_std, w_vt, cos, sina, sinb, width_a, layer):
    B, S, D = x.shape
    n3 = w_std.shape[2]
    width = n3 // 3
    grid = (B, S // PROJ_ROWS)
    row_block = pl.BlockSpec((1, PROJ_ROWS, width), lambda b, i: (b, i, 0))
    tab_block = pl.BlockSpec((PROJ_ROWS, LANES), lambda b, i: (i, 0))
    return pl.pallas_call(
        functools.partial(_in_proj_kernel, width_a=width_a),
        grid=grid,
        in_specs=[
            pl.BlockSpec((1, PROJ_ROWS, D), lambda b, i: (b, i, 0)),
            pl.BlockSpec((1, D, n3), lambda b, i: (layer, 0, 0)),
            pl.BlockSpec((1, width, D), lambda b, i: (layer, 0, 0)),
            tab_block, tab_block, tab_block,
        ],
        out_specs=[
            row_block, row_block, row_block,
            pl.BlockSpec((1, PROJ_ROWS // LANES, width, LANES), lambda b, i: (b, i, 0, 0)),
        ],
        out_shape=[
            jax.ShapeDtypeStruct((B, S, width), jnp.bfloat16),
            jax.ShapeDtypeStruct((B, S, width), jnp.bfloat16),
            jax.ShapeDtypeStruct((B, S, width), jnp.bfloat16),
            jax.ShapeDtypeStruct((B, S // LANES, width, LANES), jnp.bfloat16),
        ],
        compiler_params=pltpu.CompilerParams(
            dimension_semantics=("arbitrary", "arbitrary"), vmem_limit_bytes=VMEM_LIMIT_BYTES),
    )(x, w_std, w_vt, cos, sina, sinb)


def _attn_a_kernel(q_ref, k_ref, vt_ref, g_ref, f_ref, o_ref, bias_ref, s_ref, p_ref):
    i = pl.program_id(1)
    n_pairs = q_ref.shape[2] // LANES

    @pl.when((pl.program_id(0) == 0) & (i == 0))
    def _():
        _fill_bias_table(f_ref, bias_ref)

    lane = lax.broadcasted_iota(jnp.int32, (A_Q, LANES), 1)
    first = lane < HEAD_DIM
    ones = jnp.ones((BF16_SUBLANES, A_WIN), jnp.bfloat16)
    items = [(sb, p) for sb in range(A_STEP_BLOCKS) for p in range(n_pairs)]

    def first_key_block(sb):
        return jnp.maximum(i * A_STEP_BLOCKS + sb - A_LEFT_BLOCKS, 0)

    def scores(t):
        sb, p = items[t]
        cols = slice(p * LANES, (p + 1) * LANES)
        qp = q_ref[0, sb * A_Q:(sb + 1) * A_Q, cols]
        zero = jnp.zeros_like(qp)
        q2 = jnp.concatenate([jnp.where(first, qp, zero), jnp.where(first, zero, qp)], axis=0)
        start = pl.multiple_of(first_key_block(sb) * A_Q, A_Q)
        kw = k_ref[0, pl.ds(start, A_WIN), cols]
        s_ref[t % 2] = lax.dot_general(kw, q2, _NT, preferred_element_type=jnp.float32)

    def softmax(t):
        sb, p = items[t]
        blk0 = first_key_block(sb)
        bias_row = pl.multiple_of((A_LEFT_BLOCKS - (i * A_STEP_BLOCKS + sb - blk0)) * A_Q, A_Q)
        s = s_ref[t % 2] + bias_ref[p, pl.ds(bias_row, A_WIN), :]
        m = jnp.max(s, axis=0, keepdims=True)
        p_ref[t % 2] = jnp.exp2(s - m).astype(jnp.bfloat16)

    def finish(t):
        sb, p = items[t]
        cols = slice(p * LANES, (p + 1) * LANES)
        rows = slice(sb * A_Q, (sb + 1) * A_Q)
        blk0 = first_key_block(sb)
        vw = jnp.concatenate([vt_ref[0, blk0 + u, cols, :] for u in range(A_WIN // LANES)], axis=1)
        vw = jnp.concatenate([vw, ones], axis=0)
        r = jnp.dot(vw, p_ref[t % 2], preferred_element_type=jnp.float32)
        r = r[:LANES] * (1.0 / r[LANES:LANES + 1])
        ot = jnp.concatenate([r[:HEAD_DIM, :A_Q], r[HEAD_DIM:, A_Q:]], axis=0)
        g = g_ref[0, rows, cols].astype(jnp.float32)
        o_ref[0, rows, cols] = (ot.T * _silu(g)).astype(o_ref.dtype)

    n_items = len(items)
    scores(0)
    scores(1)
    softmax(0)
    for t in range(n_items):
        if t + 2 < n_items:
            scores(t + 2)
        if t + 1 < n_items:
            softmax(t + 1)
        finish(t)


def _attn_a(q_all, k_all, vt_all, g_all, bias_rows, width_a, layer):
    B, S, _ = q_all.shape
    step = A_STEP_BLOCKS * A_Q
    grid = (B, S // step)
    q_block = pl.BlockSpec((1, step, width_a), lambda b, i: (b, i, 0))
    return pl.pallas_call(
        _attn_a_kernel,
        grid=grid,
        in_specs=[
            q_block,
            pl.BlockSpec((1, S, width_a), lambda b, i: (b, 0, 0)),
            pl.BlockSpec((1, S // LANES, width_a, LANES), lambda b, i: (b, 0, 0, 0)),
            q_block,
            pl.BlockSpec((1,) + bias_rows.shape[1:], lambda b, i: (layer, 0, 0, 0)),
        ],
        out_specs=q_block,
        out_shape=jax.ShapeDtypeStruct((B, S, width_a), jnp.bfloat16),
        scratch_shapes=[
            pltpu.VMEM((width_a // LANES, A_BIAS_ROWS, 2 * A_Q), jnp.float32),
            pltpu.VMEM((2, A_WIN, 2 * A_Q), jnp.float32),
            pltpu.VMEM((2, A_WIN, 2 * A_Q), jnp.bfloat16),
        ],
        compiler_params=pltpu.CompilerParams(
            dimension_semantics=("arbitrary", "arbitrary"), vmem_limit_bytes=VMEM_LIMIT_BYTES),
    )(q_all, k_all, vt_all, g_all, bias_rows)


def _bias_rows(rel_bias):
    n_blk = A_BIAS_ROWS // A_Q
    e = np.arange(2 * A_Q)
    d = np.where(e < A_Q, e, e - 2 * A_Q)
    a = np.arange(n_blk)[:, None]
    idx = np.clip(A_LEFT_BLOCKS * A_Q - A_Q * a + d[None, :], -REL_CLIP, REL_CLIP) + REL_CLIP
    return rel_bias.astype(jnp.float32)[:, :, idx] * LOG2E


def _fill_bias_table(f_ref, bias_ref):
    _, n_heads, n_blk, _ = f_ref.shape
    row = lax.broadcasted_iota(jnp.int32, (A_Q, A_Q), 0)
    qchunk = lax.broadcasted_iota(jnp.int32, (A_Q, A_Q), 1) // CHUNK
    for h in range(n_heads):
        pair, half = divmod(h, 2)
        for a in range(n_blk):
            f = jnp.broadcast_to(f_ref[0, h, a:a + 1, :], (A_Q, 2 * A_Q))
            t = pltpu.roll(f, 0, 1, stride=1, stride_axis=0)[:, :A_Q]
            dchunk = qchunk - (row + a * A_Q) // CHUNK + N_LEFT_CHUNKS
            valid = (dchunk >= 0) & (dchunk <= N_LEFT_CHUNKS)
            bias_ref[pair, a * A_Q:(a + 1) * A_Q, half * A_Q:(half + 1) * A_Q] = jnp.where(valid, t, NEG_INF)


def _attn_b_kernel(q_ref, k_ref, vt_ref, g_ref, lq1_ref, lk1_ref, lq2_ref, lk2_ref, sub_ref,
                   o_ref, qm_ref, s_ref, bm_ref, m_ref, acc_ref, *, lam_init):
    n_q = q_ref.shape[1] // B_Q
    lane = lax.broadcasted_iota(jnp.int32, (B_Q, LANES), 1)
    first = lane < HEAD_DIM
    ones = jnp.ones((BF16_SUBLANES, B_K), jnp.bfloat16)
    blocks_per_step = B_K // LANES
    lam = (jnp.exp(jnp.sum(lq1_ref[0] * lk1_ref[0], axis=1, keepdims=True))
           - jnp.exp(jnp.sum(lq2_ref[0] * lk2_ref[0], axis=1, keepdims=True)) + lam_init)

    def scores(j, slot):
        kj = k_ref[0, pl.ds(pl.multiple_of(j * B_K, B_K), B_K), :]
        for n in range(2):
            s = lax.dot_general(kj, qm_ref[n], _NT, preferred_element_type=jnp.float32)
            s_ref[slot, n] = s
            bm_ref[slot, n] = jnp.max(s, axis=0, keepdims=True)

    def accumulate(j, slot, masked=False):
        vj = jnp.concatenate([vt_ref[0, j * blocks_per_step + t] for t in range(blocks_per_step)],
                             axis=1)
        vj = jnp.concatenate([vj, ones], axis=0)
        for n in range(2):
            s = s_ref[slot, n]
            if masked:
                kc = lax.broadcasted_iota(jnp.int32, (B_K, B_Q), 0) // CHUNK
                qc = lax.broadcasted_iota(jnp.int32, (B_K, B_Q), 1) // CHUNK
                s = jnp.where(kc <= qc, s, NEG_INF)
                block_max = jnp.max(s, axis=0, keepdims=True)
            else:
                block_max = bm_ref[slot, n]
            m_old = m_ref[n]
            m_new = jnp.maximum(m_old, block_max)
            alpha = jnp.exp2(m_old - m_new)
            p = jnp.exp2(s - m_new).astype(jnp.bfloat16)
            acc_ref[n] = alpha * acc_ref[n] + jnp.dot(vj, p, preferred_element_type=jnp.float32)
            m_ref[n] = m_new

    def two_blocks(jj, carry):
        j = 2 * jj
        scores(j + 1, 1)
        accumulate(j, 0)
        scores(j + 2, 0)
        accumulate(j + 1, 1)
        return carry

    def first_scores(i):
        qp = q_ref[0, pl.ds(pl.multiple_of(i * B_Q, B_Q), B_Q), :]
        zero = jnp.zeros_like(qp)
        qm_ref[0] = jnp.where(first, qp, zero)
        qm_ref[1] = jnp.where(first, zero, qp)
        scores(0, 0)

    def query_block(i, carry):
        rows = pl.ds(pl.multiple_of(i * B_Q, B_Q), B_Q)
        acc_ref[...] = jnp.zeros_like(acc_ref)
        m_ref[...] = jnp.full(m_ref.shape, NEG_INF, jnp.float32)
        lax.fori_loop(0, i // 2, two_blocks, 0)

        @pl.when(i % 2 == 1)
        def _():
            scores(i, 1)
            accumulate(i - 1, 0)

        accumulate(i, i % 2, masked=True)
        first_scores(jnp.minimum(i + 1, n_q - 1))

        l0 = acc_ref[0, LANES:LANES + 1, :]
        l1 = acc_ref[1, LANES:LANES + 1, :]
        ot = acc_ref[0, :LANES, :] * (1.0 / l0) - lam * (acc_ref[1, :LANES, :] * (1.0 / l1))
        ot = ot * lax.rsqrt(jnp.mean(ot * ot, axis=0, keepdims=True) + RMS_EPS)
        o = ot.T * (sub_ref[0] * (1.0 - lam_init))
        g = g_ref[0, rows, :].astype(jnp.float32)
        o_ref[0, rows, :] = (o * _silu(g)).astype(o_ref.dtype)
        return carry

    first_scores(0)
    lax.fori_loop(0, n_q, query_block, 0)


def _attn_b(q_all, k_all, vt_all, g_all, lq1, lk1, lq2, lk2, subln, width_a, lam_init, layer):
    B, S, width = q_all.shape
    n_heads = (width - width_a) // LANES
    col0 = width_a // LANES
    grid = (B, n_heads)
    seq_block = pl.BlockSpec((1, S, LANES), lambda b, h: (b, 0, col0 + h))
    vec64 = pl.BlockSpec((1, 1, HEAD_DIM), lambda b, h: (layer, 0, 0))
    return pl.pallas_call(
        functools.partial(_attn_b_kernel, lam_init=lam_init),
        grid=grid,
        in_specs=[
            seq_block,
            seq_block,
            pl.BlockSpec((1, S // LANES, LANES, LANES), lambda b, h: (b, 0, col0 + h, 0)),
            seq_block,
            vec64, vec64, vec64, vec64,
            pl.BlockSpec((1, 1, LANES), lambda b, h: (layer, 0, 0)),
        ],
        out_specs=pl.BlockSpec((1, S, LANES), lambda b, h: (b, 0, h)),
        out_shape=jax.ShapeDtypeStruct((B, S, width - width_a), jnp.bfloat16),
        scratch_shapes=[
            pltpu.VMEM((2, B_Q, LANES), jnp.bfloat16),
            pltpu.VMEM((2, 2, B_K, B_Q), jnp.float32),
            pltpu.VMEM((2, 2, 1, B_Q), jnp.float32),
            pltpu.VMEM((2, 1, B_Q), jnp.float32),
            pltpu.VMEM((2, ACC_ROWS, B_Q), jnp.float32),
        ],
        compiler_params=pltpu.CompilerParams(
            dimension_semantics=("arbitrary", "arbitrary"), vmem_limit_bytes=VMEM_LIMIT_BYTES),
    )(q_all, k_all, vt_all, g_all, lq1, lk1, lq2, lk2, subln)


def _out_proj_kernel(x_ref, ya_ref, yb_ref, w_ref, g_ref, b_ref, o_ref, *, alpha):
    width_a = ya_ref.shape[2]
    for r in range(x_ref.shape[1] // OUT_SUB_ROWS):
        rows = slice(r * OUT_SUB_ROWS, (r + 1) * OUT_SUB_ROWS)
        y = jnp.dot(ya_ref[0, rows, :], w_ref[0, :width_a, :], preferred_element_type=jnp.float32)
        y = y + jnp.dot(yb_ref[0, rows, :], w_ref[0, width_a:, :], preferred_element_type=jnp.float32)
        z = alpha * x_ref[0, rows, :] + y
        mu = jnp.mean(z, axis=1, keepdims=True)
        zc = z - mu
        var = jnp.mean(zc * zc, axis=1, keepdims=True)
        o_ref[0, rows, :] = zc * lax.rsqrt(var + LN_EPS) * g_ref[0] + b_ref[0]


def _out_proj(x, ya, yb, w, ln_g, ln_b, alpha, layer):
    B, S, D = x.shape
    grid = (B, S // PROJ_ROWS)
    x_block = pl.BlockSpec((1, PROJ_ROWS, D), lambda b, i: (b, i, 0))
    vec = pl.BlockSpec((1, 1, D), lambda b, i: (layer, 0, 0))
    return pl.pallas_call(
        functools.partial(_out_proj_kernel, alpha=alpha),
        grid=grid,
        in_specs=[
            x_block,
            pl.BlockSpec((1, PROJ_ROWS, ya.shape[2]), lambda b, i: (b, i, 0)),
            pl.BlockSpec((1, PROJ_ROWS, yb.shape[2]), lambda b, i: (b, i, 0)),
            pl.BlockSpec((1,) + w.shape[1:], lambda b, i: (layer, 0, 0)),
            vec, vec,
        ],
        out_specs=x_block,
        out_shape=jax.ShapeDtypeStruct((B, S, D), jnp.float32),
        compiler_params=pltpu.CompilerParams(
            dimension_semantics=("arbitrary", "arbitrary"), vmem_limit_bytes=VMEM_LIMIT_BYTES),
    )(x, ya, yb, w, ln_g, ln_b)


def _rope_tables(seq):
    half = HEAD_DIM // 2
    inv_freq = ROPE_THETA ** (-jnp.arange(0, HEAD_DIM, 2, dtype=jnp.float32) / HEAD_DIM)
    pos = jnp.arange(seq, dtype=jnp.float32)
    ang = pos[:, None] * inv_freq[None, :]
    ang = jnp.concatenate([ang] * (LANES // half), axis=-1)
    low = (np.arange(LANES) % HEAD_DIM) < half
    sin = jnp.sin(ang)
    return jnp.cos(ang), jnp.where(low, -sin, 0.0), jnp.where(low, 0.0, sin)


def kernel(x, w_in, w_out, rel_bias, lambda_q1, lambda_k1, lambda_q2, lambda_k2, subln_g, ln_g, ln_b):
    depth, d_model, proj_cols = w_in.shape
    seq = x.shape[1]
    width_a = proj_cols // 8
    alpha = (2 * depth) ** 0.25
    scale = HEAD_DIM ** -0.5
    cos, sina, sinb = _rope_tables(seq)
    w4 = w_in.reshape(depth, d_model, 8, width_a)
    q_scale = scale * LOG2E
    w_std = jnp.concatenate([w4[:, :, 0] * q_scale, w4[:, :, 4] * q_scale, w4[:, :, 1], w4[:, :, 5],
                             w4[:, :, 3], w4[:, :, 7]], axis=-1).astype(jnp.bfloat16)
    w_vt = jnp.concatenate([w4[:, :, 2], w4[:, :, 6]], axis=-1).transpose(0, 2, 1).astype(jnp.bfloat16)
    w_o = w_out.astype(jnp.bfloat16)
    bias_rows = _bias_rows(rel_bias)
    lq1, lk1, lq2, lk2, subln, g_ln, b_ln = (
        p.astype(jnp.float32)[:, None, :] for p in (lambda_q1, lambda_k1, lambda_q2, lambda_k2, subln_g, ln_g, ln_b))
    for l in range(depth):
        q_all, k_all, g_all, vt_all = _in_proj(x, w_std, w_vt, cos, sina, sinb, width_a, l)
        ya = _attn_a(q_all, k_all, vt_all, g_all, bias_rows, width_a, l)
        lam_init = 0.8 - 0.6 * math.exp(-0.3 * l)
        yb = _attn_b(q_all, k_all, vt_all, g_all, lq1, lk1, lq2, lk2, subln, width_a, lam_init, l)
        x = _out_proj(x, ya, yb, w_o, g_ln, b_ln, alpha, l)
    return x
```

```python
import functools
import math

import numpy as np
import jax
import jax.numpy as jnp
from jax import lax
from jax.experimental import pallas as pl
from jax.experimental.pallas import tpu as pltpu

CHUNK = 64
N_LEFT_CHUNKS = 8
HEAD_DIM = 64
REL_CLIP = 128
ROPE_THETA = 10000.0
LN_EPS = 1e-5
RMS_EPS = 1e-5
NEG_INF = -1e30

LANES = 128
VMEM_LIMIT_BYTES = 52 * 1024 * 1024

PROJ_ROWS = 512
PROJ_COLS_CHUNK = 512
OUT_SUB_ROWS = 256
A_Q = 128
A_WIN = A_Q + N_LEFT_CHUNKS * CHUNK
A_LEFT_BLOCKS = N_LEFT_CHUNKS * CHUNK // A_Q
A_BIAS_ROWS = A_WIN + A_LEFT_BLOCKS * A_Q
A_STEP_BLOCKS = 8
B_Q = 512
B_K = 512
BF16_SUBLANES = 16
ACC_ROWS = LANES + BF16_SUBLANES
LOG2E = 1.4426950408889634

_NT = (((1,), (1,)), ((), ()))


def _silu(g):
    return g * (1.0 / (1.0 + jnp.exp(-g)))


def _in_proj_kernel(x_ref, w_ref, wv_ref, cos_ref, sina_ref, sinb_ref,
                    q_ref, k_ref, g_ref, vt_ref, wvt_ref, *, width_a):
    @pl.when((pl.program_id(0) == 0) & (pl.program_id(1) == 0))
    def _():
        for c in range(wv_ref.shape[2] // LANES):
            cols = slice(c * LANES, (c + 1) * LANES)
            wvt_ref[cols, :] = wv_ref[0, :, cols].astype(jnp.float32).T.astype(jnp.bfloat16)

    xb = x_ref[0].astype(jnp.bfloat16)
    n_std = w_ref.shape[2] // PROJ_COLS_CHUNK
    per_kind = n_std // 3
    outs = (q_ref, k_ref, g_ref)
    cos = cos_ref[...]
    sina = sina_ref[...]
    sinb = sinb_ref[...]
    for c in range(n_std):
        kind, sub = divmod(c, per_kind)
        col0 = sub * PROJ_COLS_CHUNK
        r = jnp.dot(xb, w_ref[0, :, c * PROJ_COLS_CHUNK:(c + 1) * PROJ_COLS_CHUNK],
                    preferred_element_type=jnp.float32)
        rotary = kind < 2 and col0 >= width_a
        for t in range(PROJ_COLS_CHUNK // LANES):
            rt = r[:, t * LANES:(t + 1) * LANES]
            if rotary:
                rt = (rt * cos + pltpu.roll(rt, LANES - HEAD_DIM // 2, 1) * sina
                      + pltpu.roll(rt, HEAD_DIM // 2, 1) * sinb)
            outs[kind][0, :, col0 + t * LANES:col0 + (t + 1) * LANES] = rt.astype(jnp.bfloat16)
    rows = xb.shape[0]
    n_v = wvt_ref.shape[0] // PROJ_COLS_CHUNK
    for c in range(n_v):
        rt = lax.dot_general(wvt_ref[c * PROJ_COLS_CHUNK:(c + 1) * PROJ_COLS_CHUNK, :], xb, _NT,
                             preferred_element_type=jnp.float32)
        for j in range(rows // LANES):
            vt_ref[0, j, c * PROJ_COLS_CHUNK:(c + 1) * PROJ_COLS_CHUNK, :] = (
                rt[:, j * LANES:(j + 1) * LANES].astype(jnp.bfloat16))


def _in_proj(x, w_std, w_v, cos, sina, sinb, width_a, layer):
    B, S, D = x.shape
    n3 = w_std.shape[2]
    width = n3 // 3
    grid = (B, S // PROJ_ROWS)
    row_block = pl.BlockSpec((1, PROJ_ROWS, width), lambda b, i: (b, i, 0))
    tab_block = pl.BlockSpec((PROJ_ROWS, LANES), lambda b, i: (i, 0))
    return pl.pallas_call(
        functools.partial(_in_proj_kernel, width_a=width_a),
        grid=grid,
        in_specs=[
            pl.BlockSpec((1, PROJ_ROWS, D), lambda b, i: (b, i, 0)),
            pl.BlockSpec((1, D, n3), lambda b, i: (layer, 0, 0)),
            pl.BlockSpec((1, D, width), lambda b, i: (layer, 0, 0)),
            tab_block, tab_block, tab_block,
        ],
        out_specs=[
            row_block, row_block, row_block,
            pl.BlockSpec((1, PROJ_ROWS // LANES, width, LANES), lambda b, i: (b, i, 0, 0)),
        ],
        out_shape=[
            jax.ShapeDtypeStruct((B, S, width), jnp.bfloat16),
            jax.ShapeDtypeStruct((B, S, width), jnp.bfloat16),
            jax.ShapeDtypeStruct((B, S, width), jnp.bfloat16),
            jax.ShapeDtypeStruct((B, S // LANES, width, LANES), jnp.bfloat16),
        ],
        scratch_shapes=[pltpu.VMEM((width, D), jnp.bfloat16)],
        compiler_params=pltpu.CompilerParams(
            dimension_semantics=("arbitrary", "arbitrary"), vmem_limit_bytes=VMEM_LIMIT_BYTES),
    )(x, w_std, w_v, cos, sina, sinb)


def _attn_a_kernel(q_ref, k_ref, vt_ref, g_ref, f_ref, o_ref, bias_ref, s_ref, p_ref):
    i = pl.program_id(1)
    n_pairs = q_ref.shape[2] // LANES

    @pl.when((pl.program_id(0) == 0) & (i == 0))
    def _():
        _fill_bias_table(f_ref, bias_ref)

    lane = lax.broadcasted_iota(jnp.int32, (A_Q, LANES), 1)
    first = lane < HEAD_DIM
    ones = jnp.ones((BF16_SUBLANES, A_WIN), jnp.bfloat16)
    items = [(sb, p) for sb in range(A_STEP_BLOCKS) for p in range(n_pairs)]

    def first_key_block(sb):
        return jnp.maximum(i * A_STEP_BLOCKS + sb - A_LEFT_BLOCKS, 0)

    def scores(t):
        sb, p = items[t]
        cols = slice(p * LANES, (p + 1) * LANES)
        qp = q_ref[0, sb * A_Q:(sb + 1) * A_Q, cols]
        zero = jnp.zeros_like(qp)
        q2 = jnp.concatenate([jnp.where(first, qp, zero), jnp.where(first, zero, qp)], axis=0)
        start = pl.multiple_of(first_key_block(sb) * A_Q, A_Q)
        kw = k_ref[0, pl.ds(start, A_WIN), cols]
        s_ref[t % 2] = lax.dot_general(kw, q2, _NT, preferred_element_type=jnp.float32)

    def softmax(t):
        sb, p = items[t]
        blk0 = first_key_block(sb)
        bias_row = pl.multiple_of((A_LEFT_BLOCKS - (i * A_STEP_BLOCKS + sb - blk0)) * A_Q, A_Q)
        s = s_ref[t % 2] + bias_ref[p, pl.ds(bias_row, A_WIN), :]
        m = jnp.max(s, axis=0, keepdims=True)
        p_ref[t % 2] = jnp.exp2(s - m).astype(jnp.bfloat16)

    def finish(t):
        sb, p = items[t]
        cols = slice(p * LANES, (p + 1) * LANES)
        rows = slice(sb * A_Q, (sb + 1) * A_Q)
        blk0 = first_key_block(sb)
        vw = jnp.concatenate([vt_ref[0, blk0 + u, cols, :] for u in range(A_WIN // LANES)], axis=1)
        vw = jnp.concatenate([vw, ones], axis=0)
        r = jnp.dot(vw, p_ref[t % 2], preferred_element_type=jnp.float32)
        r = r[:LANES] * (1.0 / r[LANES:LANES + 1])
        ot = jnp.concatenate([r[:HEAD_DIM, :A_Q], r[HEAD_DIM:, A_Q:]], axis=0)
        g = g_ref[0, rows, cols].astype(jnp.float32)
        o_ref[0, rows, cols] = (ot.T * _silu(g)).astype(o_ref.dtype)

    n_items = len(items)
    scores(0)
    scores(1)
    softmax(0)
    for t in range(n_items):
        if t + 2 < n_items:
            scores(t + 2)
        if t + 1 < n_items:
            softmax(t + 1)
        finish(t)


def _attn_a(q_all, k_all, vt_all, g_all, bias_rows, width_a, layer):
    B, S, _ = q_all.shape
    step = A_STEP_BLOCKS * A_Q
    grid = (B, S // step)
    q_block = pl.BlockSpec((1, step, width_a), lambda b, i: (b, i, 0))
    return pl.pallas_call(
        _attn_a_kernel,
        grid=grid,
        in_specs=[
            q_block,
            pl.BlockSpec((1, S, width_a), lambda b, i: (b, 0, 0)),
            pl.BlockSpec((1, S // LANES, width_a, LANES), lambda b, i: (b, 0, 0, 0)),
            q_block,
            pl.BlockSpec((1,) + bias_rows.shape[1:], lambda b, i: (layer, 0, 0, 0)),
        ],
        out_specs=q_block,
        out_shape=jax.ShapeDtypeStruct((B, S, width_a), jnp.bfloat16),
        scratch_shapes=[
            pltpu.VMEM((width_a // LANES, A_BIAS_ROWS, 2 * A_Q), jnp.float32),
            pltpu.VMEM((2, A_WIN, 2 * A_Q), jnp.float32),
            pltpu.VMEM((2, A_WIN, 2 * A_Q), jnp.bfloat16),
        ],
        compiler_params=pltpu.CompilerParams(
            dimension_semantics=("arbitrary", "arbitrary"), vmem_limit_bytes=VMEM_LIMIT_BYTES),
    )(q_all, k_all, vt_all, g_all, bias_rows)


def _bias_rows(rel_bias):
    n_blk = A_BIAS_ROWS // A_Q
    e = np.arange(2 * A_Q)
    d = np.where(e < A_Q, e, e - 2 * A_Q)
    a = np.arange(n_blk)[:, None]
    idx = np.clip(A_LEFT_BLOCKS * A_Q - A_Q * a + d[None, :], -REL_CLIP, REL_CLIP) + REL_CLIP
    return rel_bias.astype(jnp.float32)[:, :, idx] * LOG2E


def _fill_bias_table(f_ref, bias_ref):
    _, n_heads, n_blk, _ = f_ref.shape
    row = lax.broadcasted_iota(jnp.int32, (A_Q, A_Q), 0)
    qchunk = lax.broadcasted_iota(jnp.int32, (A_Q, A_Q), 1) // CHUNK
    for h in range(n_heads):
        pair, half = divmod(h, 2)
        for a in range(n_blk):
            f = jnp.broadcast_to(f_ref[0, h, a:a + 1, :], (A_Q, 2 * A_Q))
            t = pltpu.roll(f, 0, 1, stride=1, stride_axis=0)[:, :A_Q]
            dchunk = qchunk - (row + a * A_Q) // CHUNK + N_LEFT_CHUNKS
            valid = (dchunk >= 0) & (dchunk <= N_LEFT_CHUNKS)
            bias_ref[pair, a * A_Q:(a + 1) * A_Q, half * A_Q:(half + 1) * A_Q] = jnp.where(valid, t, NEG_INF)


def _attn_b_kernel(q_ref, k_ref, vt_ref, g_ref, lq1_ref, lk1_ref, lq2_ref, lk2_ref, sub_ref,
                   o_ref, qm_ref, s_ref, bm_ref, m_ref, acc_ref, *, lam_init):
    n_q = q_ref.shape[1] // B_Q
    lane = lax.broadcasted_iota(jnp.int32, (B_Q, LANES), 1)
    first = lane < HEAD_DIM
    ones = jnp.ones((BF16_SUBLANES, B_K), jnp.bfloat16)
    blocks_per_step = B_K // LANES
    lam = (jnp.exp(jnp.sum(lq1_ref[0] * lk1_ref[0], axis=1, keepdims=True))
           - jnp.exp(jnp.sum(lq2_ref[0] * lk2_ref[0], axis=1, keepdims=True)) + lam_init)

    def scores(j, slot):
        kj = k_ref[0, pl.ds(pl.multiple_of(j * B_K, B_K), B_K), :]
        for n in range(2):
            s = lax.dot_general(kj, qm_ref[n], _NT, preferred_element_type=jnp.float32)
            s_ref[slot, n, :, :B_Q] = s
            bm_ref[slot, n] = jnp.max(s, axis=0, keepdims=True)

    def accumulate(j, slot, masked=False):
        vj = jnp.concatenate([vt_ref[0, j * blocks_per_step + t] for t in range(blocks_per_step)],
                             axis=1)
        vj = jnp.concatenate([vj, ones], axis=0)
        for n in range(2):
            s = s_ref[slot, n, :, :B_Q]
            if masked:
                kc = lax.broadcasted_iota(jnp.int32, (B_K, B_Q), 0) // CHUNK
                qc = lax.broadcasted_iota(jnp.int32, (B_K, B_Q), 1) // CHUNK
                s = jnp.where(kc <= qc, s, NEG_INF)
                block_max = jnp.max(s, axis=0, keepdims=True)
            else:
                block_max = bm_ref[slot, n]
            m_old = m_ref[n]
            m_new = jnp.maximum(m_old, block_max)
            alpha = jnp.exp2(m_old - m_new)
            p = jnp.exp2(s - m_new).astype(jnp.bfloat16)
            acc_ref[n] = alpha * acc_ref[n] + jnp.dot(vj, p, preferred_element_type=jnp.float32)
            m_ref[n] = m_new

    def two_blocks(jj, carry):
        j = 2 * jj
        scores(j + 1, 1)
        accumulate(j, 0)
        scores(j + 2, 0)
        accumulate(j + 1, 1)
        return carry

    def first_scores(i):
        qp = q_ref[0, pl.ds(pl.multiple_of(i * B_Q, B_Q), B_Q), :]
        zero = jnp.zeros_like(qp)
        qm_ref[0] = jnp.where(first, qp, zero)
        qm_ref[1] = jnp.where(first, zero, qp)
        scores(0, 0)

    def query_block(i, carry):
        rows = pl.ds(pl.multiple_of(i * B_Q, B_Q), B_Q)
        acc_ref[...] = jnp.zeros_like(acc_ref)
        m_ref[...] = jnp.full(m_ref.shape, NEG_INF, jnp.float32)
        lax.fori_loop(0, i // 2, two_blocks, 0)

        @pl.when(i % 2 == 1)
        def _():
            scores(i, 1)
            accumulate(i - 1, 0)

        accumulate(i, i % 2, masked=True)
        first_scores(jnp.minimum(i + 1, n_q - 1))

        l0 = acc_ref[0, LANES:LANES + 1, :]
        l1 = acc_ref[1, LANES:LANES + 1, :]
        ot = acc_ref[0, :LANES, :] * (1.0 / l0) - lam * (acc_ref[1, :LANES, :] * (1.0 / l1))
        ot = ot * lax.rsqrt(jnp.mean(ot * ot, axis=0, keepdims=True) + RMS_EPS)
        o = ot.T * (sub_ref[0] * (1.0 - lam_init))
        g = g_ref[0, rows, :].astype(jnp.float32)
        o_ref[0, rows, :] = (o * _silu(g)).astype(o_ref.dtype)
        return carry

    first_scores(0)
    lax.fori_loop(0, n_q, query_block, 0)


def _attn_b(q_all, k_all, vt_all, g_all, lq1, lk1, lq2, lk2, subln, width_a, lam_init, layer):
    B, S, width = q_all.shape
    n_heads = (width - width_a) // LANES
    col0 = width_a // LANES
    grid = (B, n_heads)
    seq_block = pl.BlockSpec((1, S, LANES), lambda b, h: (b, 0, col0 + h))
    vec64 = pl.BlockSpec((1, 1, HEAD_DIM), lambda b, h: (layer, 0, 0))
    return pl.pallas_call(
        functools.partial(_attn_b_kernel, lam_init=lam_init),
        grid=grid,
        in_specs=[
            seq_block,
            seq_block,
            pl.BlockSpec((1, S // LANES, LANES, LANES), lambda b, h: (b, 0, col0 + h, 0)),
            seq_block,
            vec64, vec64, vec64, vec64,
            pl.BlockSpec((1, 1, LANES), lambda b, h: (layer, 0, 0)),
        ],
        out_specs=pl.BlockSpec((1, S, LANES), lambda b, h: (b, 0, h)),
        out_shape=jax.ShapeDtypeStruct((B, S, width - width_a), jnp.bfloat16),
        scratch_shapes=[
            pltpu.VMEM((2, B_Q, LANES), jnp.bfloat16),
            pltpu.VMEM((2, 2, B_K, B_Q + LANES), jnp.float32),
            pltpu.VMEM((2, 2, 1, B_Q), jnp.float32),
            pltpu.VMEM((2, 1, B_Q), jnp.float32),
            pltpu.VMEM((2, ACC_ROWS, B_Q), jnp.float32),
        ],
        compiler_params=pltpu.CompilerParams(
            dimension_semantics=("arbitrary", "arbitrary"), vmem_limit_bytes=VMEM_LIMIT_BYTES),
    )(q_all, k_all, vt_all, g_all, lq1, lk1, lq2, lk2, subln)


def _out_proj_kernel(x_ref, ya_ref, yb_ref, w_ref, g_ref, b_ref, o_ref, *, alpha):
    width_a = ya_ref.shape[2]
    for r in range(x_ref.shape[1] // OUT_SUB_ROWS):
        rows = slice(r * OUT_SUB_ROWS, (r + 1) * OUT_SUB_ROWS)
        y = jnp.dot(ya_ref[0, rows, :], w_ref[0, :width_a, :], preferred_element_type=jnp.float32)
        y = y + jnp.dot(yb_ref[0, rows, :], w_ref[0, width_a:, :], preferred_element_type=jnp.float32)
        z = alpha * x_ref[0, rows, :] + y
        mu = jnp.mean(z, axis=1, keepdims=True)
        zc = z - mu
        var = jnp.mean(zc * zc, axis=1, keepdims=True)
        o_ref[0, rows, :] = zc * lax.rsqrt(var + LN_EPS) * g_ref[0] + b_ref[0]


def _out_proj(x, ya, yb, w, ln_g, ln_b, alpha, layer):
    B, S, D = x.shape
    grid = (B, S // PROJ_ROWS)
    x_block = pl.BlockSpec((1, PROJ_ROWS, D), lambda b, i: (b, i, 0))
    vec = pl.BlockSpec((1, 1, D), lambda b, i: (layer, 0, 0))
    return pl.pallas_call(
        functools.partial(_out_proj_kernel, alpha=alpha),
        grid=grid,
        in_specs=[
            x_block,
            pl.BlockSpec((1, PROJ_ROWS, ya.shape[2]), lambda b, i: (b, i, 0)),
            pl.BlockSpec((1, PROJ_ROWS, yb.shape[2]), lambda b, i: (b, i, 0)),
            pl.BlockSpec((1,) + w.shape[1:], lambda b, i: (layer, 0, 0)),
            vec, vec,
        ],
        out_specs=x_block,
        out_shape=jax.ShapeDtypeStruct((B, S, D), jnp.float32),
        compiler_params=pltpu.CompilerParams(
            dimension_semantics=("arbitrary", "arbitrary"), vmem_limit_bytes=VMEM_LIMIT_BYTES),
    )(x, ya, yb, w, ln_g, ln_b)


def _rope_tables(seq):
    half = HEAD_DIM // 2
    inv_freq = ROPE_THETA ** (-jnp.arange(0, HEAD_DIM, 2, dtype=jnp.float32) / HEAD_DIM)
    pos = jnp.arange(seq, dtype=jnp.float32)
    ang = pos[:, None] * inv_freq[None, :]
    ang = jnp.concatenate([ang] * (LANES // half), axis=-1)
    low = (np.arange(LANES) % HEAD_DIM) < half
    sin = jnp.sin(ang)
    return jnp.cos(ang), jnp.where(low, -sin, 0.0), jnp.where(low, 0.0, sin)


def kernel(x, w_in, w_out, rel_bias, lambda_q1, lambda_k1, lambda_q2, lambda_k2, subln_g, ln_g, ln_b):
    depth, d_model, proj_cols = w_in.shape
    seq = x.shape[1]
    width_a = proj_cols // 8
    alpha = (2 * depth) ** 0.25
    scale = HEAD_DIM ** -0.5
    cos, sina, sinb = _rope_tables(seq)
    w4 = w_in.reshape(depth, d_model, 8, width_a)
    q_scale = scale * LOG2E
    w_std = jnp.concatenate([w4[:, :, 0] * q_scale, w4[:, :, 4] * q_scale, w4[:, :, 1], w4[:, :, 5],
                             w4[:, :, 3], w4[:, :, 7]], axis=-1).astype(jnp.bfloat16)
    w_v = jnp.concatenate([w4[:, :, 2], w4[:, :, 6]], axis=-1).astype(jnp.bfloat16)
    w_o = w_out.astype(jnp.bfloat16)
    bias_rows = _bias_rows(rel_bias)
    lq1, lk1, lq2, lk2, subln, g_ln, b_ln = (
        p.astype(jnp.float32)[:, None, :] for p in (lambda_q1, lambda_k1, lambda_q2, lambda_k2, subln_g, ln_g, ln_b))
    for l in range(depth):
        q_all, k_all, g_all, vt_all = _in_proj(x, w_std, w_v, cos, sina, sinb, width_a, l)
        ya = _attn_a(q_all, k_all, vt_all, g_all, bias_rows, width_a, l)
        lam_init = 0.8 - 0.6 * math.exp(-0.3 * l)
        yb = _attn_b(q_all, k_all, vt_all, g_all, lq1, lk1, lq2, lk2, subln, width_a, lam_init, l)
        x = _out_proj(x, ya, yb, w_o, g_ln, b_ln, alpha, l)
    return x
```

```python
import functools
import math

import numpy as np
import jax
import jax.numpy as jnp
from jax import lax
from jax.experimental import pallas as pl
from jax.experimental.pallas import tpu as pltpu

CHUNK = 64
N_LEFT_CHUNKS = 8
HEAD_DIM = 64
REL_CLIP = 128
ROPE_THETA = 10000.0
LN_EPS = 1e-5
RMS_EPS = 1e-5
NEG_INF = -1e30

LANES = 128
VMEM_LIMIT_BYTES = 52 * 1024 * 1024

PROJ_ROWS = 512
PROJ_COLS_CHUNK = 512
OUT_ROWS = 1024
OUT_SUB_ROWS = 256
A_Q = 128
A_WIN = A_Q + N_LEFT_CHUNKS * CHUNK
A_LEFT_BLOCKS = N_LEFT_CHUNKS * CHUNK // A_Q
A_BIAS_ROWS = A_WIN + A_LEFT_BLOCKS * A_Q
A_STEP_BLOCKS = 8
B_Q = 512
B_K = 512
BF16_SUBLANES = 16
ACC_ROWS = LANES + BF16_SUBLANES
LOG2E = 1.4426950408889634

_NT = (((1,), (1,)), ((), ()))


def _silu(g):
    return g * (1.0 / (1.0 + jnp.exp(-g)))


_STD_BLOCKS = (0, 4, 1, 5, 3, 7)
_V_BLOCKS = (2, 6)


def _in_proj_kernel(x_ref, w_ref, cos_ref, sina_ref, sinb_ref,
                    q_ref, k_ref, g_ref, vt_ref, wstd_ref, wvt_ref, *, width_a, q_scale):
    @pl.when((pl.program_id(0) == 0) & (pl.program_id(1) == 0))
    def _():
        for c, blk in enumerate(_STD_BLOCKS):
            w = w_ref[0, :, blk * width_a:(blk + 1) * width_a]
            if c < 2:
                w = w * q_scale
            wstd_ref[:, c * width_a:(c + 1) * width_a] = w.astype(jnp.bfloat16)
        for c, blk in enumerate(_V_BLOCKS):
            for t in range(width_a // LANES):
                src = slice(blk * width_a + t * LANES, blk * width_a + (t + 1) * LANES)
                dst = slice(c * width_a + t * LANES, c * width_a + (t + 1) * LANES)
                wvt_ref[dst, :] = w_ref[0, :, src].T.astype(jnp.bfloat16)

    xb = x_ref[0].astype(jnp.bfloat16)
    n_std = wstd_ref.shape[1] // PROJ_COLS_CHUNK
    per_kind = n_std // 3
    outs = (q_ref, k_ref, g_ref)
    cos = cos_ref[...]
    sina = sina_ref[...]
    sinb = sinb_ref[...]
    for c in range(n_std):
        kind, sub = divmod(c, per_kind)
        col0 = sub * PROJ_COLS_CHUNK
        r = jnp.dot(xb, wstd_ref[:, c * PROJ_COLS_CHUNK:(c + 1) * PROJ_COLS_CHUNK],
                    preferred_element_type=jnp.float32)
        rotary = kind < 2 and col0 >= width_a
        for t in range(PROJ_COLS_CHUNK // LANES):
            rt = r[:, t * LANES:(t + 1) * LANES]
            if rotary:
                rt = (rt * cos + pltpu.roll(rt, LANES - HEAD_DIM // 2, 1) * sina
                      + pltpu.roll(rt, HEAD_DIM // 2, 1) * sinb)
            outs[kind][0, :, col0 + t * LANES:col0 + (t + 1) * LANES] = rt.astype(jnp.bfloat16)
    rows = xb.shape[0]
    n_v = wvt_ref.shape[0] // PROJ_COLS_CHUNK
    for c in range(n_v):
        rt = lax.dot_general(wvt_ref[c * PROJ_COLS_CHUNK:(c + 1) * PROJ_COLS_CHUNK, :], xb, _NT,
                             preferred_element_type=jnp.float32)
        for j in range(rows // LANES):
            vt_ref[0, j, c * PROJ_COLS_CHUNK:(c + 1) * PROJ_COLS_CHUNK, :] = (
                rt[:, j * LANES:(j + 1) * LANES].astype(jnp.bfloat16))


def _in_proj(x, w_in, cos, sina, sinb, width_a, q_scale, layer):
    B, S, D = x.shape
    proj_cols = w_in.shape[2]
    width = len(_V_BLOCKS) * width_a
    grid = (B, S // PROJ_ROWS)
    row_block = pl.BlockSpec((1, PROJ_ROWS, width), lambda b, i: (b, i, 0))
    tab_block = pl.BlockSpec((PROJ_ROWS, LANES), lambda b, i: (i, 0))
    return pl.pallas_call(
        functools.partial(_in_proj_kernel, width_a=width_a, q_scale=q_scale),
        grid=grid,
        in_specs=[
            pl.BlockSpec((1, PROJ_ROWS, D), lambda b, i: (b, i, 0)),
            pl.BlockSpec((1, D, proj_cols), lambda b, i: (layer, 0, 0), pipeline_mode=pl.Buffered(1)),
            tab_block, tab_block, tab_block,
        ],
        out_specs=[
            row_block, row_block, row_block,
            pl.BlockSpec((1, PROJ_ROWS // LANES, width, LANES), lambda b, i: (b, i, 0, 0)),
        ],
        out_shape=[
            jax.ShapeDtypeStruct((B, S, width), jnp.bfloat16),
            jax.ShapeDtypeStruct((B, S, width), jnp.bfloat16),
            jax.ShapeDtypeStruct((B, S, width), jnp.bfloat16),
            jax.ShapeDtypeStruct((B, S // LANES, width, LANES), jnp.bfloat16),
        ],
        scratch_shapes=[
            pltpu.VMEM((D, len(_STD_BLOCKS) * width_a), jnp.bfloat16),
            pltpu.VMEM((width, D), jnp.bfloat16),
        ],
        compiler_params=pltpu.CompilerParams(
            dimension_semantics=("arbitrary", "arbitrary"), vmem_limit_bytes=VMEM_LIMIT_BYTES),
    )(x, w_in, cos, sina, sinb)


def _attn_a_kernel(q_ref, k_ref, vt_ref, g_ref, f_ref, o_ref, bias_ref, s_ref, p_ref):
    i = pl.program_id(1)
    n_pairs = q_ref.shape[2] // LANES

    @pl.when((pl.program_id(0) == 0) & (i == 0))
    def _():
        _fill_bias_table(f_ref, bias_ref)

    lane = lax.broadcasted_iota(jnp.int32, (A_Q, LANES), 1)
    first = lane < HEAD_DIM
    ones = jnp.ones((BF16_SUBLANES, A_WIN), jnp.bfloat16)
    items = [(sb, p) for sb in range(A_STEP_BLOCKS) for p in range(n_pairs)]

    def first_key_block(sb):
        return jnp.maximum(i * A_STEP_BLOCKS + sb - A_LEFT_BLOCKS, 0)

    def scores(t):
        sb, p = items[t]
        cols = slice(p * LANES, (p + 1) * LANES)
        qp = q_ref[0, sb * A_Q:(sb + 1) * A_Q, cols]
        zero = jnp.zeros_like(qp)
        q2 = jnp.concatenate([jnp.where(first, qp, zero), jnp.where(first, zero, qp)], axis=0)
        start = pl.multiple_of(first_key_block(sb) * A_Q, A_Q)
        kw = k_ref[0, pl.ds(start, A_WIN), cols]
        s_ref[t % 2] = lax.dot_general(kw, q2, _NT, preferred_element_type=jnp.float32)

    def softmax(t):
        sb, p = items[t]
        blk0 = first_key_block(sb)
        bias_row = pl.multiple_of((A_LEFT_BLOCKS - (i * A_STEP_BLOCKS + sb - blk0)) * A_Q, A_Q)
        s = s_ref[t % 2] + bias_ref[p, pl.ds(bias_row, A_WIN), :]
        m = jnp.max(s, axis=0, keepdims=True)
        p_ref[t % 2] = jnp.exp2(s - m).astype(jnp.bfloat16)

    def finish(t):
        sb, p = items[t]
        cols = slice(p * LANES, (p + 1) * LANES)
        rows = slice(sb * A_Q, (sb + 1) * A_Q)
        blk0 = first_key_block(sb)
        vw = jnp.concatenate([vt_ref[0, blk0 + u, cols, :] for u in range(A_WIN // LANES)], axis=1)
        vw = jnp.concatenate([vw, ones], axis=0)
        r = jnp.dot(vw, p_ref[t % 2], preferred_element_type=jnp.float32)
        r = r[:LANES] * (1.0 / r[LANES:LANES + 1])
        ot = jnp.concatenate([r[:HEAD_DIM, :A_Q], r[HEAD_DIM:, A_Q:]], axis=0)
        g = g_ref[0, rows, cols].astype(jnp.float32)
        o_ref[0, rows, cols] = (ot.T * _silu(g)).astype(o_ref.dtype)

    n_items = len(items)
    scores(0)
    scores(1)
    softmax(0)
    for t in range(n_items):
        if t + 2 < n_items:
            scores(t + 2)
        if t + 1 < n_items:
            softmax(t + 1)
        finish(t)


def _attn_a(q_all, k_all, vt_all, g_all, bias_rows, width_a, layer):
    B, S, _ = q_all.shape
    step = A_STEP_BLOCKS * A_Q
    grid = (B, S // step)
    q_block = pl.BlockSpec((1, step, width_a), lambda b, i: (b, i, 0))
    return pl.pallas_call(
        _attn_a_kernel,
        grid=grid,
        in_specs=[
            q_block,
            pl.BlockSpec((1, S, width_a), lambda b, i: (b, 0, 0)),
            pl.BlockSpec((1, S // LANES, width_a, LANES), lambda b, i: (b, 0, 0, 0)),
            q_block,
            pl.BlockSpec((1,) + bias_rows.shape[1:], lambda b, i: (layer, 0, 0, 0)),
        ],
        out_specs=q_block,
        out_shape=jax.ShapeDtypeStruct((B, S, width_a), jnp.bfloat16),
        scratch_shapes=[
            pltpu.VMEM((width_a // LANES, A_BIAS_ROWS, 2 * A_Q), jnp.float32),
            pltpu.VMEM((2, A_WIN, 2 * A_Q), jnp.float32),
            pltpu.VMEM((2, A_WIN, 2 * A_Q), jnp.bfloat16),
        ],
        compiler_params=pltpu.CompilerParams(
            dimension_semantics=("arbitrary", "arbitrary"), vmem_limit_bytes=VMEM_LIMIT_BYTES),
    )(q_all, k_all, vt_all, g_all, bias_rows)


def _bias_rows(rel_bias):
    n_blk = A_BIAS_ROWS // A_Q
    e = np.arange(2 * A_Q)
    d = np.where(e < A_Q, e, e - 2 * A_Q)
    a = np.arange(n_blk)[:, None]
    idx = np.clip(A_LEFT_BLOCKS * A_Q - A_Q * a + d[None, :], -REL_CLIP, REL_CLIP) + REL_CLIP
    return rel_bias.astype(jnp.float32)[:, :, idx] * LOG2E


def _fill_bias_table(f_ref, bias_ref):
    _, n_heads, n_blk, _ = f_ref.shape
    row = lax.broadcasted_iota(jnp.int32, (A_Q, A_Q), 0)
    qchunk = lax.broadcasted_iota(jnp.int32, (A_Q, A_Q), 1) // CHUNK
    for h in range(n_heads):
        pair, half = divmod(h, 2)
        for a in range(n_blk):
            f = jnp.broadcast_to(f_ref[0, h, a:a + 1, :], (A_Q, 2 * A_Q))
            t = pltpu.roll(f, 0, 1, stride=1, stride_axis=0)[:, :A_Q]
            dchunk = qchunk - (row + a * A_Q) // CHUNK + N_LEFT_CHUNKS
            valid = (dchunk >= 0) & (dchunk <= N_LEFT_CHUNKS)
            bias_ref[pair, a * A_Q:(a + 1) * A_Q, half * A_Q:(half + 1) * A_Q] = jnp.where(valid, t, NEG_INF)


def _attn_b_kernel(q_ref, k_ref, vt_ref, g_ref, lq1_ref, lk1_ref, lq2_ref, lk2_ref, sub_ref,
                   o_ref, qm_ref, s_ref, bm_ref, m_ref, acc_ref, *, lam_init):
    n_q = q_ref.shape[1] // B_Q
    lane = lax.broadcasted_iota(jnp.int32, (B_Q, LANES), 1)
    first = lane < HEAD_DIM
    ones = jnp.ones((BF16_SUBLANES, B_K), jnp.bfloat16)
    blocks_per_step = B_K // LANES
    lam = (jnp.exp(jnp.sum(lq1_ref[0] * lk1_ref[0], axis=1, keepdims=True))
           - jnp.exp(jnp.sum(lq2_ref[0] * lk2_ref[0], axis=1, keepdims=True)) + lam_init)

    def scores(j, slot):
        kj = k_ref[0, pl.ds(pl.multiple_of(j * B_K, B_K), B_K), :]
        for n in range(2):
            s = lax.dot_general(kj, qm_ref[n], _NT, preferred_element_type=jnp.float32)
            s_ref[slot, n] = s
            bm_ref[slot, n] = jnp.max(s, axis=0, keepdims=True)

    def accumulate(j, slot, masked=False):
        vj = jnp.concatenate([vt_ref[0, j * blocks_per_step + t] for t in range(blocks_per_step)],
                             axis=1)
        vj = jnp.concatenate([vj, ones], axis=0)
        for n in range(2):
            s = s_ref[slot, n]
            if masked:
                kc = lax.broadcasted_iota(jnp.int32, (B_K, B_Q), 0) // CHUNK
                qc = lax.broadcasted_iota(jnp.int32, (B_K, B_Q), 1) // CHUNK
                s = jnp.where(kc <= qc, s, NEG_INF)
                block_max = jnp.max(s, axis=0, keepdims=True)
            else:
                block_max = bm_ref[slot, n]
            m_old = m_ref[n]
            m_new = jnp.maximum(m_old, block_max)
            alpha = jnp.exp2(m_old - m_new)
            p = jnp.exp2(s - m_new).astype(jnp.bfloat16)
            acc_ref[n] = alpha * acc_ref[n] + jnp.dot(vj, p, preferred_element_type=jnp.float32)
            m_ref[n] = m_new

    def two_blocks(jj, carry):
        j = 2 * jj
        scores(j + 1, 1)
        accumulate(j, 0)
        scores(j + 2, 0)
        accumulate(j + 1, 1)
        return carry

    def first_scores(i):
        qp = q_ref[0, pl.ds(pl.multiple_of(i * B_Q, B_Q), B_Q), :]
        zero = jnp.zeros_like(qp)
        qm_ref[0] = jnp.where(first, qp, zero)
        qm_ref[1] = jnp.where(first, zero, qp)
        scores(0, 0)

    def query_block(i, carry):
        rows = pl.ds(pl.multiple_of(i * B_Q, B_Q), B_Q)
        acc_ref[...] = jnp.zeros_like(acc_ref)
        m_ref[...] = jnp.full(m_ref.shape, NEG_INF, jnp.float32)
        lax.fori_loop(0, i // 2, two_blocks, 0)

        @pl.when(i % 2 == 1)
        def _():
            scores(i, 1)
            accumulate(i - 1, 0)

        accumulate(i, i % 2, masked=True)
        first_scores(jnp.minimum(i + 1, n_q - 1))

        l0 = acc_ref[0, LANES:LANES + 1, :]
        l1 = acc_ref[1, LANES:LANES + 1, :]
        ot = acc_ref[0, :LANES, :] * (1.0 / l0) - lam * (acc_ref[1, :LANES, :] * (1.0 / l1))
        ot = ot * lax.rsqrt(jnp.mean(ot * ot, axis=0, keepdims=True) + RMS_EPS)
        o = ot.T * (sub_ref[0] * (1.0 - lam_init))
        g = g_ref[0, rows, :].astype(jnp.float32)
        o_ref[0, rows, :] = (o * _silu(g)).astype(o_ref.dtype)
        return carry

    first_scores(0)
    lax.fori_loop(0, n_q, query_block, 0)


def _attn_b(q_all, k_all, vt_all, g_all, lq1, lk1, lq2, lk2, subln, width_a, lam_init, layer):
    B, S, width = q_all.shape
    n_heads = (width - width_a) // LANES
    col0 = width_a // LANES
    grid = (B, n_heads)
    seq_block = pl.BlockSpec((1, S, LANES), lambda b, h: (b, 0, col0 + h))
    vec64 = pl.BlockSpec((1, 1, HEAD_DIM), lambda b, h: (layer, 0, 0))
    return pl.pallas_call(
        functools.partial(_attn_b_kernel, lam_init=lam_init),
        grid=grid,
        in_specs=[
            seq_block,
            seq_block,
            pl.BlockSpec((1, S // LANES, LANES, LANES), lambda b, h: (b, 0, col0 + h, 0)),
            seq_block,
            vec64, vec64, vec64, vec64,
            pl.BlockSpec((1, 1, LANES), lambda b, h: (layer, 0, 0)),
        ],
        out_specs=pl.BlockSpec((1, S, LANES), lambda b, h: (b, 0, h)),
        out_shape=jax.ShapeDtypeStruct((B, S, width - width_a), jnp.bfloat16),
        scratch_shapes=[
            pltpu.VMEM((2, B_Q, LANES), jnp.bfloat16),
            pltpu.VMEM((2, 2, B_K, B_Q), jnp.float32),
            pltpu.VMEM((2, 2, 1, B_Q), jnp.float32),
            pltpu.VMEM((2, 1, B_Q), jnp.float32),
            pltpu.VMEM((2, ACC_ROWS, B_Q), jnp.float32),
        ],
        compiler_params=pltpu.CompilerParams(
            dimension_semantics=("arbitrary", "arbitrary"), vmem_limit_bytes=VMEM_LIMIT_BYTES),
    )(q_all, k_all, vt_all, g_all, lq1, lk1, lq2, lk2, subln)


def _out_proj_kernel(x_ref, ya_ref, yb_ref, w_ref, g_ref, b_ref, o_ref, wb_ref, *, alpha):
    width_a = ya_ref.shape[2]

    @pl.when((pl.program_id(0) == 0) & (pl.program_id(1) == 0))
    def _():
        wb_ref[...] = w_ref[0].astype(jnp.bfloat16)

    for r in range(x_ref.shape[1] // OUT_SUB_ROWS):
        rows = slice(r * OUT_SUB_ROWS, (r + 1) * OUT_SUB_ROWS)
        y = jnp.dot(ya_ref[0, rows, :], wb_ref[:width_a, :], preferred_element_type=jnp.float32)
        y = y + jnp.dot(yb_ref[0, rows, :], wb_ref[width_a:, :], preferred_element_type=jnp.float32)
        z = alpha * x_ref[0, rows, :] + y
        mu = jnp.mean(z, axis=1, keepdims=True)
        zc = z - mu
        var = jnp.mean(zc * zc, axis=1, keepdims=True)
        o_ref[0, rows, :] = zc * lax.rsqrt(var + LN_EPS) * g_ref[0] + b_ref[0]


def _out_proj(x, ya, yb, w, ln_g, ln_b, alpha, layer):
    B, S, D = x.shape
    grid = (B, S // OUT_ROWS)
    x_block = pl.BlockSpec((1, OUT_ROWS, D), lambda b, i: (b, i, 0))
    vec = pl.BlockSpec((1, 1, D), lambda b, i: (layer, 0, 0))
    return pl.pallas_call(
        functools.partial(_out_proj_kernel, alpha=alpha),
        grid=grid,
        in_specs=[
            x_block,
            pl.BlockSpec((1, OUT_ROWS, ya.shape[2]), lambda b, i: (b, i, 0)),
            pl.BlockSpec((1, OUT_ROWS, yb.shape[2]), lambda b, i: (b, i, 0)),
            pl.BlockSpec((1,) + w.shape[1:], lambda b, i: (layer, 0, 0), pipeline_mode=pl.Buffered(1)),
            vec, vec,
        ],
        out_specs=x_block,
        out_shape=jax.ShapeDtypeStruct((B, S, D), jnp.float32),
        scratch_shapes=[pltpu.VMEM(w.shape[1:], jnp.bfloat16)],
        compiler_params=pltpu.CompilerParams(
            dimension_semantics=("arbitrary", "arbitrary"), vmem_limit_bytes=VMEM_LIMIT_BYTES),
    )(x, ya, yb, w, ln_g, ln_b)


def _rope_tables(seq):
    half = HEAD_DIM // 2
    inv_freq = ROPE_THETA ** (-jnp.arange(0, HEAD_DIM, 2, dtype=jnp.float32) / HEAD_DIM)
    pos = jnp.arange(seq, dtype=jnp.float32)
    ang = pos[:, None] * inv_freq[None, :]
    ang = jnp.concatenate([ang] * (LANES // half), axis=-1)
    low = (np.arange(LANES) % HEAD_DIM) < half
    sin = jnp.sin(ang)
    return jnp.cos(ang), jnp.where(low, -sin, 0.0), jnp.where(low, 0.0, sin)


def kernel(x, w_in, w_out, rel_bias, lambda_q1, lambda_k1, lambda_q2, lambda_k2, subln_g, ln_g, ln_b):
    depth, d_model, proj_cols = w_in.shape
    seq = x.shape[1]
    width_a = proj_cols // 8
    alpha = (2 * depth) ** 0.25
    scale = HEAD_DIM ** -0.5
    cos, sina, sinb = _rope_tables(seq)
    q_scale = scale * LOG2E
    bias_rows = _bias_rows(rel_bias)
    lq1, lk1, lq2, lk2, subln, g_ln, b_ln = (
        p.astype(jnp.float32)[:, None, :] for p in (lambda_q1, lambda_k1, lambda_q2, lambda_k2, subln_g, ln_g, ln_b))
    for l in range(depth):
        q_all, k_all, g_all, vt_all = _in_proj(x, w_in, cos, sina, sinb, width_a, q_scale, l)
        ya = _attn_a(q_all, k_all, vt_all, g_all, bias_rows, width_a, l)
        lam_init = 0.8 - 0.6 * math.exp(-0.3 * l)
        yb = _attn_b(q_all, k_all, vt_all, g_all, lq1, lk1, lq2, lk2, subln, width_a, lam_init, l)
        x = _out_proj(x, ya, yb, w_out, g_ln, b_ln, alpha, l)
    return x
```

```python
import functools
import math

import numpy as np
import jax
import jax.numpy as jnp
from jax import lax
from jax.experimental import pallas as pl
from jax.experimental.pallas import tpu as pltpu

CHUNK = 64
N_LEFT_CHUNKS = 8
HEAD_DIM = 64
REL_CLIP = 128
ROPE_THETA = 10000.0
LN_EPS = 1e-5
RMS_EPS = 1e-5
NEG_INF = -1e30

LANES = 128
VMEM_LIMIT_BYTES = 52 * 1024 * 1024

PROJ_ROWS = 512
PROJ_COLS_CHUNK = 512
OUT_ROWS = 1024
OUT_SUB_ROWS = 256
A_Q = 128
A_WIN = A_Q + N_LEFT_CHUNKS * CHUNK
A_LEFT_BLOCKS = N_LEFT_CHUNKS * CHUNK // A_Q
A_BIAS_ROWS = A_WIN + A_LEFT_BLOCKS * A_Q
A_STEP_BLOCKS = 8
B_Q = 512
B_K = 512
B_HEADS = 2
BF16_SUBLANES = 16
ACC_ROWS = LANES + BF16_SUBLANES
LOG2E = 1.4426950408889634

_NT = (((1,), (1,)), ((), ()))


def _silu(g):
    return g * (1.0 / (1.0 + jnp.exp(-g)))


_STD_BLOCKS = (0, 4, 1, 5, 3, 7)
_V_BLOCKS = (2, 6)


def _in_proj_kernel(x_ref, w_ref, cos_ref, sina_ref, sinb_ref,
                    q_ref, k_ref, g_ref, vt_ref, wstd_ref, wvt_ref, *, width_a, q_scale):
    @pl.when((pl.program_id(0) == 0) & (pl.program_id(1) == 0))
    def _():
        for c, blk in enumerate(_STD_BLOCKS):
            w = w_ref[0, :, blk * width_a:(blk + 1) * width_a]
            if c < 2:
                w = w * q_scale
            wstd_ref[:, c * width_a:(c + 1) * width_a] = w.astype(jnp.bfloat16)
        for c, blk in enumerate(_V_BLOCKS):
            for t in range(width_a // LANES):
                src = slice(blk * width_a + t * LANES, blk * width_a + (t + 1) * LANES)
                dst = slice(c * width_a + t * LANES, c * width_a + (t + 1) * LANES)
                wvt_ref[dst, :] = w_ref[0, :, src].T.astype(jnp.bfloat16)

    xb = x_ref[0].astype(jnp.bfloat16)
    n_std = wstd_ref.shape[1] // PROJ_COLS_CHUNK
    per_kind = n_std // 3
    outs = (q_ref, k_ref, g_ref)
    cos = cos_ref[...]
    sina = sina_ref[...]
    sinb = sinb_ref[...]
    for c in range(n_std):
        kind, sub = divmod(c, per_kind)
        col0 = sub * PROJ_COLS_CHUNK
        r = jnp.dot(xb, wstd_ref[:, c * PROJ_COLS_CHUNK:(c + 1) * PROJ_COLS_CHUNK],
                    preferred_element_type=jnp.float32)
        rotary = kind < 2 and col0 >= width_a
        for t in range(PROJ_COLS_CHUNK // LANES):
            rt = r[:, t * LANES:(t + 1) * LANES]
            if rotary:
                rt = (rt * cos + pltpu.roll(rt, LANES - HEAD_DIM // 2, 1) * sina
                      + pltpu.roll(rt, HEAD_DIM // 2, 1) * sinb)
            outs[kind][0, :, col0 + t * LANES:col0 + (t + 1) * LANES] = rt.astype(jnp.bfloat16)
    rows = xb.shape[0]
    n_v = wvt_ref.shape[0] // PROJ_COLS_CHUNK
    for c in range(n_v):
        rt = lax.dot_general(wvt_ref[c * PROJ_COLS_CHUNK:(c + 1) * PROJ_COLS_CHUNK, :], xb, _NT,
                             preferred_element_type=jnp.float32)
        for j in range(rows // LANES):
            vt_ref[0, j, c * PROJ_COLS_CHUNK:(c + 1) * PROJ_COLS_CHUNK, :] = (
                rt[:, j * LANES:(j + 1) * LANES].astype(jnp.bfloat16))


def _in_proj(x, w_in, cos, sina, sinb, width_a, q_scale, layer):
    B, S, D = x.shape
    proj_cols = w_in.shape[2]
    width = len(_V_BLOCKS) * width_a
    grid = (B, S // PROJ_ROWS)
    row_block = pl.BlockSpec((1, PROJ_ROWS, width), lambda b, i: (b, i, 0))
    tab_block = pl.BlockSpec((PROJ_ROWS, LANES), lambda b, i: (i, 0))
    return pl.pallas_call(
        functools.partial(_in_proj_kernel, width_a=width_a, q_scale=q_scale),
        grid=grid,
        in_specs=[
            pl.BlockSpec((1, PROJ_ROWS, D), lambda b, i: (b, i, 0)),
            pl.BlockSpec((1, D, proj_cols), lambda b, i: (layer, 0, 0), pipeline_mode=pl.Buffered(1)),
            tab_block, tab_block, tab_block,
        ],
        out_specs=[
            row_block, row_block, row_block,
            pl.BlockSpec((1, PROJ_ROWS // LANES, width, LANES), lambda b, i: (b, i, 0, 0)),
        ],
        out_shape=[
            jax.ShapeDtypeStruct((B, S, width), jnp.bfloat16),
            jax.ShapeDtypeStruct((B, S, width), jnp.bfloat16),
            jax.ShapeDtypeStruct((B, S, width), jnp.bfloat16),
            jax.ShapeDtypeStruct((B, S // LANES, width, LANES), jnp.bfloat16),
        ],
        scratch_shapes=[
            pltpu.VMEM((D, len(_STD_BLOCKS) * width_a), jnp.bfloat16),
            pltpu.VMEM((width, D), jnp.bfloat16),
        ],
        compiler_params=pltpu.CompilerParams(
            dimension_semantics=("arbitrary", "arbitrary"), vmem_limit_bytes=VMEM_LIMIT_BYTES),
    )(x, w_in, cos, sina, sinb)


def _attn_a_kernel(q_ref, k_ref, vt_ref, g_ref, f_ref, o_ref, bias_ref, s_ref, p_ref):
    i = pl.program_id(1)
    n_pairs = q_ref.shape[2] // LANES

    @pl.when((pl.program_id(0) == 0) & (i == 0))
    def _():
        _fill_bias_table(f_ref, bias_ref)

    lane = lax.broadcasted_iota(jnp.int32, (A_Q, LANES), 1)
    first = lane < HEAD_DIM
    ones = jnp.ones((BF16_SUBLANES, A_WIN), jnp.bfloat16)
    items = [(sb, p) for sb in range(A_STEP_BLOCKS) for p in range(n_pairs)]

    def first_key_block(sb):
        return jnp.maximum(i * A_STEP_BLOCKS + sb - A_LEFT_BLOCKS, 0)

    def scores(t):
        sb, p = items[t]
        cols = slice(p * LANES, (p + 1) * LANES)
        qp = q_ref[0, sb * A_Q:(sb + 1) * A_Q, cols]
        zero = jnp.zeros_like(qp)
        q2 = jnp.concatenate([jnp.where(first, qp, zero), jnp.where(first, zero, qp)], axis=0)
        start = pl.multiple_of(first_key_block(sb) * A_Q, A_Q)
        kw = k_ref[0, pl.ds(start, A_WIN), cols]
        s_ref[t % 2] = lax.dot_general(kw, q2, _NT, preferred_element_type=jnp.float32)

    def softmax(t):
        sb, p = items[t]
        blk0 = first_key_block(sb)
        bias_row = pl.multiple_of((A_LEFT_BLOCKS - (i * A_STEP_BLOCKS + sb - blk0)) * A_Q, A_Q)
        s = s_ref[t % 2] + bias_ref[p, pl.ds(bias_row, A_WIN), :]
        m = jnp.max(s, axis=0, keepdims=True)
        p_ref[t % 2] = jnp.exp2(s - m).astype(jnp.bfloat16)

    def finish(t):
        sb, p = items[t]
        cols = slice(p * LANES, (p + 1) * LANES)
        rows = slice(sb * A_Q, (sb + 1) * A_Q)
        blk0 = first_key_block(sb)
        vw = jnp.concatenate([vt_ref[0, blk0 + u, cols, :] for u in range(A_WIN // LANES)], axis=1)
        vw = jnp.concatenate([vw, ones], axis=0)
        r = jnp.dot(vw, p_ref[t % 2], preferred_element_type=jnp.float32)
        r = r[:LANES] * (1.0 / r[LANES:LANES + 1])
        ot = jnp.concatenate([r[:HEAD_DIM, :A_Q], r[HEAD_DIM:, A_Q:]], axis=0)
        g = g_ref[0, rows, cols].astype(jnp.float32)
        o_ref[0, rows, cols] = (ot.T * _silu(g)).astype(o_ref.dtype)

    n_items = len(items)
    scores(0)
    scores(1)
    softmax(0)
    for t in range(n_items):
        if t + 2 < n_items:
            scores(t + 2)
        if t + 1 < n_items:
            softmax(t + 1)
        finish(t)


def _attn_a(q_all, k_all, vt_all, g_all, bias_rows, width_a, layer):
    B, S, _ = q_all.shape
    step = A_STEP_BLOCKS * A_Q
    grid = (B, S // step)
    q_block = pl.BlockSpec((1, step, width_a), lambda b, i: (b, i, 0))
    return pl.pallas_call(
        _attn_a_kernel,
        grid=grid,
        in_specs=[
            q_block,
            pl.BlockSpec((1, S, width_a), lambda b, i: (b, 0, 0)),
            pl.BlockSpec((1, S // LANES, width_a, LANES), lambda b, i: (b, 0, 0, 0)),
            q_block,
            pl.BlockSpec((1,) + bias_rows.shape[1:], lambda b, i: (layer, 0, 0, 0)),
        ],
        out_specs=q_block,
        out_shape=jax.ShapeDtypeStruct((B, S, width_a), jnp.bfloat16),
        scratch_shapes=[
            pltpu.VMEM((width_a // LANES, A_BIAS_ROWS, 2 * A_Q), jnp.float32),
            pltpu.VMEM((2, A_WIN, 2 * A_Q), jnp.float32),
            pltpu.VMEM((2, A_WIN, 2 * A_Q), jnp.bfloat16),
        ],
        compiler_params=pltpu.CompilerParams(
            dimension_semantics=("arbitrary", "arbitrary"), vmem_limit_bytes=VMEM_LIMIT_BYTES),
    )(q_all, k_all, vt_all, g_all, bias_rows)


def _bias_rows(rel_bias):
    n_blk = A_BIAS_ROWS // A_Q
    e = np.arange(2 * A_Q)
    d = np.where(e < A_Q, e, e - 2 * A_Q)
    a = np.arange(n_blk)[:, None]
    idx = np.clip(A_LEFT_BLOCKS * A_Q - A_Q * a + d[None, :], -REL_CLIP, REL_CLIP) + REL_CLIP
    return rel_bias.astype(jnp.float32)[:, :, idx] * LOG2E


def _fill_bias_table(f_ref, bias_ref):
    _, n_heads, n_blk, _ = f_ref.shape
    row = lax.broadcasted_iota(jnp.int32, (A_Q, A_Q), 0)
    qchunk = lax.broadcasted_iota(jnp.int32, (A_Q, A_Q), 1) // CHUNK
    for h in range(n_heads):
        pair, half = divmod(h, 2)
        for a in range(n_blk):
            f = jnp.broadcast_to(f_ref[0, h, a:a + 1, :], (A_Q, 2 * A_Q))
            t = pltpu.roll(f, 0, 1, stride=1, stride_axis=0)[:, :A_Q]
            dchunk = qchunk - (row + a * A_Q) // CHUNK + N_LEFT_CHUNKS
            valid = (dchunk >= 0) & (dchunk <= N_LEFT_CHUNKS)
            bias_ref[pair, a * A_Q:(a + 1) * A_Q, half * A_Q:(half + 1) * A_Q] = jnp.where(valid, t, NEG_INF)


def _attn_b_kernel(q_ref, k_ref, vt_ref, g_ref, lq1_ref, lk1_ref, lq2_ref, lk2_ref, sub_ref,
                   o_ref, qm_ref, s_ref, bm_ref, m_ref, acc_ref, *, lam_init):
    n_q = q_ref.shape[1] // B_Q
    heads = range(B_HEADS)
    lane = lax.broadcasted_iota(jnp.int32, (B_Q, LANES), 1)
    first = lane < HEAD_DIM
    ones = jnp.ones((BF16_SUBLANES, B_K), jnp.bfloat16)
    blocks_per_step = B_K // LANES
    lam = (jnp.exp(jnp.sum(lq1_ref[0] * lk1_ref[0], axis=1, keepdims=True))
           - jnp.exp(jnp.sum(lq2_ref[0] * lk2_ref[0], axis=1, keepdims=True)) + lam_init)

    def head_lanes(t):
        return slice(t * LANES, (t + 1) * LANES)

    def scores(j, slot):
        for t in heads:
            kj = k_ref[0, pl.ds(pl.multiple_of(j * B_K, B_K), B_K), head_lanes(t)]
            for n in range(2):
                s = lax.dot_general(kj, qm_ref[t, n], _NT, preferred_element_type=jnp.float32)
                s_ref[t, slot, n] = s
                bm_ref[t, slot, n] = jnp.max(s, axis=0, keepdims=True)

    def accumulate(j, slot, masked=False):
        for t in heads:
            vj = jnp.concatenate([vt_ref[0, j * blocks_per_step + u, head_lanes(t), :]
                                  for u in range(blocks_per_step)], axis=1)
            vj = jnp.concatenate([vj, ones], axis=0)
            for n in range(2):
                s = s_ref[t, slot, n]
                if masked:
                    kc = lax.broadcasted_iota(jnp.int32, (B_K, B_Q), 0) // CHUNK
                    qc = lax.broadcasted_iota(jnp.int32, (B_K, B_Q), 1) // CHUNK
                    s = jnp.where(kc <= qc, s, NEG_INF)
                    block_max = jnp.max(s, axis=0, keepdims=True)
                else:
                    block_max = bm_ref[t, slot, n]
                m_old = m_ref[t, n]
                m_new = jnp.maximum(m_old, block_max)
                alpha = jnp.exp2(m_old - m_new)
                p = jnp.exp2(s - m_new).astype(jnp.bfloat16)
                acc_ref[t, n] = alpha * acc_ref[t, n] + jnp.dot(vj, p, preferred_element_type=jnp.float32)
                m_ref[t, n] = m_new

    def two_blocks(jj, carry):
        j = 2 * jj
        scores(j + 1, 1)
        accumulate(j, 0)
        scores(j + 2, 0)
        accumulate(j + 1, 1)
        return carry

    def first_scores(i):
        rows = pl.ds(pl.multiple_of(i * B_Q, B_Q), B_Q)
        for t in heads:
            qp = q_ref[0, rows, head_lanes(t)]
            zero = jnp.zeros_like(qp)
            qm_ref[t, 0] = jnp.where(first, qp, zero)
            qm_ref[t, 1] = jnp.where(first, zero, qp)
        scores(0, 0)

    def query_block(i, carry):
        rows = pl.ds(pl.multiple_of(i * B_Q, B_Q), B_Q)
        acc_ref[...] = jnp.zeros_like(acc_ref)
        m_ref[...] = jnp.full(m_ref.shape, NEG_INF, jnp.float32)
        lax.fori_loop(0, i // 2, two_blocks, 0)

        @pl.when(i % 2 == 1)
        def _():
            scores(i, 1)
            accumulate(i - 1, 0)

        accumulate(i, i % 2, masked=True)
        first_scores(jnp.minimum(i + 1, n_q - 1))

        for t in heads:
            l0 = acc_ref[t, 0, LANES:LANES + 1, :]
            l1 = acc_ref[t, 1, LANES:LANES + 1, :]
            ot = (acc_ref[t, 0, :LANES, :] * (1.0 / l0)
                  - lam * (acc_ref[t, 1, :LANES, :] * (1.0 / l1)))
            ot = ot * lax.rsqrt(jnp.mean(ot * ot, axis=0, keepdims=True) + RMS_EPS)
            o = ot.T * (sub_ref[0] * (1.0 - lam_init))
            g = g_ref[0, rows, head_lanes(t)].astype(jnp.float32)
            o_ref[0, rows, head_lanes(t)] = (o * _silu(g)).astype(o_ref.dtype)
        return carry

    first_scores(0)
    lax.fori_loop(0, n_q, query_block, 0)


def _attn_b(q_all, k_all, vt_all, g_all, lq1, lk1, lq2, lk2, subln, width_a, lam_init, layer):
    B, S, width = q_all.shape
    step_lanes = B_HEADS * LANES
    col0 = width_a // step_lanes
    grid = (B, (width - width_a) // step_lanes)
    seq_block = pl.BlockSpec((1, S, step_lanes), lambda b, h: (b, 0, col0 + h))
    vec64 = pl.BlockSpec((1, 1, HEAD_DIM), lambda b, h: (layer, 0, 0))
    return pl.pallas_call(
        functools.partial(_attn_b_kernel, lam_init=lam_init),
        grid=grid,
        in_specs=[
            seq_block,
            seq_block,
            pl.BlockSpec((1, S // LANES, step_lanes, LANES), lambda b, h: (b, 0, col0 + h, 0)),
            seq_block,
            vec64, vec64, vec64, vec64,
            pl.BlockSpec((1, 1, LANES), lambda b, h: (layer, 0, 0)),
        ],
        out_specs=pl.BlockSpec((1, S, step_lanes), lambda b, h: (b, 0, h)),
        out_shape=jax.ShapeDtypeStruct((B, S, width - width_a), jnp.bfloat16),
        scratch_shapes=[
            pltpu.VMEM((B_HEADS, 2, B_Q, LANES), jnp.bfloat16),
            pltpu.VMEM((B_HEADS, 2, 2, B_K, B_Q), jnp.float32),
            pltpu.VMEM((B_HEADS, 2, 2, 1, B_Q), jnp.float32),
            pltpu.VMEM((B_HEADS, 2, 1, B_Q), jnp.float32),
            pltpu.VMEM((B_HEADS, 2, ACC_ROWS, B_Q), jnp.float32),
        ],
        compiler_params=pltpu.CompilerParams(
            dimension_semantics=("arbitrary", "arbitrary"), vmem_limit_bytes=VMEM_LIMIT_BYTES),
    )(q_all, k_all, vt_all, g_all, lq1, lk1, lq2, lk2, subln)


def _out_proj_kernel(x_ref, ya_ref, yb_ref, w_ref, g_ref, b_ref, o_ref, wb_ref, *, alpha):
    width_a = ya_ref.shape[2]

    @pl.when((pl.program_id(0) == 0) & (pl.program_id(1) == 0))
    def _():
        wb_ref[...] = w_ref[0].astype(jnp.bfloat16)

    for r in range(x_ref.shape[1] // OUT_SUB_ROWS):
        rows = slice(r * OUT_SUB_ROWS, (r + 1) * OUT_SUB_ROWS)
        y = jnp.dot(ya_ref[0, rows, :], wb_ref[:width_a, :], preferred_element_type=jnp.float32)
        y = y + jnp.dot(yb_ref[0, rows, :], wb_ref[width_a:, :], preferred_element_type=jnp.float32)
        z = alpha * x_ref[0, rows, :] + y
        mu = jnp.mean(z, axis=1, keepdims=True)
        zc = z - mu
        var = jnp.mean(zc * zc, axis=1, keepdims=True)
        o_ref[0, rows, :] = zc * lax.rsqrt(var + LN_EPS) * g_ref[0] + b_ref[0]


def _out_proj(x, ya, yb, w, ln_g, ln_b, alpha, layer):
    B, S, D = x.shape
    grid = (B, S // OUT_ROWS)
    x_block = pl.BlockSpec((1, OUT_ROWS, D), lambda b, i: (b, i, 0))
    vec = pl.BlockSpec((1, 1, D), lambda b, i: (layer, 0, 0))
    return pl.pallas_call(
        functools.partial(_out_proj_kernel, alpha=alpha),
        grid=grid,
        in_specs=[
            x_block,
            pl.BlockSpec((1, OUT_ROWS, ya.shape[2]), lambda b, i: (b, i, 0)),
            pl.BlockSpec((1, OUT_ROWS, yb.shape[2]), lambda b, i: (b, i, 0)),
            pl.BlockSpec((1,) + w.shape[1:], lambda b, i: (layer, 0, 0), pipeline_mode=pl.Buffered(1)),
            vec, vec,
        ],
        out_specs=x_block,
        out_shape=jax.ShapeDtypeStruct((B, S, D), jnp.float32),
        scratch_shapes=[pltpu.VMEM(w.shape[1:], jnp.bfloat16)],
        compiler_params=pltpu.CompilerParams(
            dimension_semantics=("arbitrary", "arbitrary"), vmem_limit_bytes=VMEM_LIMIT_BYTES),
    )(x, ya, yb, w, ln_g, ln_b)


def _rope_tables(seq):
    half = HEAD_DIM // 2
    inv_freq = ROPE_THETA ** (-jnp.arange(0, HEAD_DIM, 2, dtype=jnp.float32) / HEAD_DIM)
    pos = jnp.arange(seq, dtype=jnp.float32)
    ang = pos[:, None] * inv_freq[None, :]
    ang = jnp.concatenate([ang] * (LANES // half), axis=-1)
    low = (np.arange(LANES) % HEAD_DIM) < half
    sin = jnp.sin(ang)
    return jnp.cos(ang), jnp.where(low, -sin, 0.0), jnp.where(low, 0.0, sin)


def kernel(x, w_in, w_out, rel_bias, lambda_q1, lambda_k1, lambda_q2, lambda_k2, subln_g, ln_g, ln_b):
    depth, d_model, proj_cols = w_in.shape
    seq = x.shape[1]
    width_a = proj_cols // 8
    alpha = (2 * depth) ** 0.25
    scale = HEAD_DIM ** -0.5
    cos, sina, sinb = _rope_tables(seq)
    q_scale = scale * LOG2E
    bias_rows = _bias_rows(rel_bias)
    lq1, lk1, lq2, lk2, subln, g_ln, b_ln = (
        p.astype(jnp.float32)[:, None, :] for p in (lambda_q1, lambda_k1, lambda_q2, lambda_k2, subln_g, ln_g, ln_b))
    for l in range(depth):
        q_all, k_all, g_all, vt_all = _in_proj(x, w_in, cos, sina, sinb, width_a, q_scale, l)
        ya = _attn_a(q_all, k_all, vt_all, g_all, bias_rows, width_a, l)
        lam_init = 0.8 - 0.6 * math.exp(-0.3 * l)
        yb = _attn_b(q_all, k_all, vt_all, g_all, lq1, lk1, lq2, lk2, subln, width_a, lam_init, l)
        x = _out_proj(x, ya, yb, w_out, g_ln, b_ln, alpha, l)
    return x
```

```python
import functools
import math

import numpy as np
import jax
import jax.numpy as jnp
from jax import lax
from jax.experimental import pallas as pl
from jax.experimental.pallas import tpu as pltpu

CHUNK = 64
N_LEFT_CHUNKS = 8
HEAD_DIM = 64
REL_CLIP = 128
ROPE_THETA = 10000.0
LN_EPS = 1e-5
RMS_EPS = 1e-5
NEG_INF = -1e30

LANES = 128
VMEM_LIMIT_BYTES = 52 * 1024 * 1024

PROJ_ROWS = 512
PROJ_COLS_CHUNK = 512
OUT_ROWS = 2048
OUT_SUB_ROWS = 256
A_Q = 128
A_WIN = A_Q + N_LEFT_CHUNKS * CHUNK
A_LEFT_BLOCKS = N_LEFT_CHUNKS * CHUNK // A_Q
A_BIAS_ROWS = A_WIN + A_LEFT_BLOCKS * A_Q
A_STEP_BLOCKS = 8
B_Q = 512
B_K = 512
B_HEADS = 2
BF16_SUBLANES = 16
ACC_ROWS = LANES + BF16_SUBLANES
LOG2E = 1.4426950408889634

_NT = (((1,), (1,)), ((), ()))


def _silu(g):
    return g * (1.0 / (1.0 + jnp.exp(-g)))


_STD_BLOCKS = (0, 4, 1, 5, 3, 7)
_V_BLOCKS = (2, 6)


def _in_proj_kernel(x_ref, w_ref, cos_ref, sina_ref, sinb_ref,
                    q_ref, k_ref, g_ref, vt_ref, wstd_ref, wvt_ref, *, width_a, q_scale):
    @pl.when((pl.program_id(0) == 0) & (pl.program_id(1) == 0))
    def _():
        for c, blk in enumerate(_STD_BLOCKS):
            w = w_ref[0, :, blk * width_a:(blk + 1) * width_a]
            if c < 2:
                w = w * q_scale
            wstd_ref[:, c * width_a:(c + 1) * width_a] = w.astype(jnp.bfloat16)
        for c, blk in enumerate(_V_BLOCKS):
            for t in range(width_a // LANES):
                src = slice(blk * width_a + t * LANES, blk * width_a + (t + 1) * LANES)
                dst = slice(c * width_a + t * LANES, c * width_a + (t + 1) * LANES)
                wvt_ref[dst, :] = w_ref[0, :, src].T.astype(jnp.bfloat16)

    xb = x_ref[0].astype(jnp.bfloat16)
    n_std = wstd_ref.shape[1] // PROJ_COLS_CHUNK
    per_kind = n_std // 3
    outs = (q_ref, k_ref, g_ref)
    cos = cos_ref[...]
    sina = sina_ref[...]
    sinb = sinb_ref[...]
    for c in range(n_std):
        kind, sub = divmod(c, per_kind)
        col0 = sub * PROJ_COLS_CHUNK
        r = jnp.dot(xb, wstd_ref[:, c * PROJ_COLS_CHUNK:(c + 1) * PROJ_COLS_CHUNK],
                    preferred_element_type=jnp.float32)
        rotary = kind < 2 and col0 >= width_a
        for t in range(PROJ_COLS_CHUNK // LANES):
            rt = r[:, t * LANES:(t + 1) * LANES]
            if rotary:
                rt = (rt * cos + pltpu.roll(rt, LANES - HEAD_DIM // 2, 1) * sina
                      + pltpu.roll(rt, HEAD_DIM // 2, 1) * sinb)
            outs[kind][0, :, col0 + t * LANES:col0 + (t + 1) * LANES] = rt.astype(jnp.bfloat16)
    rows = xb.shape[0]
    n_v = wvt_ref.shape[0] // PROJ_COLS_CHUNK
    for c in range(n_v):
        rt = lax.dot_general(wvt_ref[c * PROJ_COLS_CHUNK:(c + 1) * PROJ_COLS_CHUNK, :], xb, _NT,
                             preferred_element_type=jnp.float32)
        for j in range(rows // LANES):
            vt_ref[0, j, c * PROJ_COLS_CHUNK:(c + 1) * PROJ_COLS_CHUNK, :] = (
                rt[:, j * LANES:(j + 1) * LANES].astype(jnp.bfloat16))


def _in_proj(x, w_in, cos, sina, sinb, width_a, q_scale, layer):
    B, S, D = x.shape
    proj_cols = w_in.shape[2]
    width = len(_V_BLOCKS) * width_a
    grid = (B, S // PROJ_ROWS)
    row_block = pl.BlockSpec((1, PROJ_ROWS, width), lambda b, i: (b, i, 0))
    tab_block = pl.BlockSpec((PROJ_ROWS, LANES), lambda b, i: (i, 0))
    return pl.pallas_call(
        functools.partial(_in_proj_kernel, width_a=width_a, q_scale=q_scale),
        grid=grid,
        in_specs=[
            pl.BlockSpec((1, PROJ_ROWS, D), lambda b, i: (b, i, 0)),
            pl.BlockSpec((1, D, proj_cols), lambda b, i: (layer, 0, 0), pipeline_mode=pl.Buffered(1)),
            tab_block, tab_block, tab_block,
        ],
        out_specs=[
            row_block, row_block, row_block,
            pl.BlockSpec((1, PROJ_ROWS // LANES, width, LANES), lambda b, i: (b, i, 0, 0)),
        ],
        out_shape=[
            jax.ShapeDtypeStruct((B, S, width), jnp.bfloat16),
            jax.ShapeDtypeStruct((B, S, width), jnp.bfloat16),
            jax.ShapeDtypeStruct((B, S, width), jnp.bfloat16),
            jax.ShapeDtypeStruct((B, S // LANES, width, LANES), jnp.bfloat16),
        ],
        scratch_shapes=[
            pltpu.VMEM((D, len(_STD_BLOCKS) * width_a), jnp.bfloat16),
            pltpu.VMEM((width, D), jnp.bfloat16),
        ],
        compiler_params=pltpu.CompilerParams(
            dimension_semantics=("arbitrary", "arbitrary"), vmem_limit_bytes=VMEM_LIMIT_BYTES),
    )(x, w_in, cos, sina, sinb)


def _attn_a_kernel(q_ref, k_ref, vt_ref, g_ref, f_ref, o_ref, bias_ref, s_ref, p_ref):
    i = pl.program_id(1)
    n_pairs = q_ref.shape[2] // LANES

    @pl.when((pl.program_id(0) == 0) & (i == 0))
    def _():
        _fill_bias_table(f_ref, bias_ref)

    lane = lax.broadcasted_iota(jnp.int32, (A_Q, LANES), 1)
    first = lane < HEAD_DIM
    ones = jnp.ones((BF16_SUBLANES, A_WIN), jnp.bfloat16)
    items = [(sb, p) for sb in range(A_STEP_BLOCKS) for p in range(n_pairs)]

    def first_key_block(sb):
        return jnp.maximum(i * A_STEP_BLOCKS + sb - A_LEFT_BLOCKS, 0)

    def scores(t):
        sb, p = items[t]
        cols = slice(p * LANES, (p + 1) * LANES)
        qp = q_ref[0, sb * A_Q:(sb + 1) * A_Q, cols]
        zero = jnp.zeros_like(qp)
        q2 = jnp.concatenate([jnp.where(first, qp, zero), jnp.where(first, zero, qp)], axis=0)
        start = pl.multiple_of(first_key_block(sb) * A_Q, A_Q)
        kw = k_ref[0, pl.ds(start, A_WIN), cols]
        s_ref[t % 2] = lax.dot_general(kw, q2, _NT, preferred_element_type=jnp.float32)

    def softmax(t):
        sb, p = items[t]
        blk0 = first_key_block(sb)
        bias_row = pl.multiple_of((A_LEFT_BLOCKS - (i * A_STEP_BLOCKS + sb - blk0)) * A_Q, A_Q)
        s = s_ref[t % 2] + bias_ref[p, pl.ds(bias_row, A_WIN), :]
        m = jnp.max(s, axis=0, keepdims=True)
        p_ref[t % 2] = jnp.exp2(s - m).astype(jnp.bfloat16)

    def finish(t):
        sb, p = items[t]
        cols = slice(p * LANES, (p + 1) * LANES)
        rows = slice(sb * A_Q, (sb + 1) * A_Q)
        blk0 = first_key_block(sb)
        vw = jnp.concatenate([vt_ref[0, blk0 + u, cols, :] for u in range(A_WIN // LANES)], axis=1)
        vw = jnp.concatenate([vw, ones], axis=0)
        r = jnp.dot(vw, p_ref[t % 2], preferred_element_type=jnp.float32)
        r = r[:LANES] * (1.0 / r[LANES:LANES + 1])
        ot = jnp.concatenate([r[:HEAD_DIM, :A_Q], r[HEAD_DIM:, A_Q:]], axis=0)
        g = g_ref[0, rows, cols].astype(jnp.float32)
        o_ref[0, rows, cols] = (ot.T * _silu(g)).astype(o_ref.dtype)

    n_items = len(items)
    scores(0)
    scores(1)
    softmax(0)
    for t in range(n_items):
        if t + 2 < n_items:
            scores(t + 2)
        if t + 1 < n_items:
            softmax(t + 1)
        finish(t)


def _attn_a(q_all, k_all, vt_all, g_all, bias_rows, width_a, layer):
    B, S, _ = q_all.shape
    step = A_STEP_BLOCKS * A_Q
    grid = (B, S // step)
    q_block = pl.BlockSpec((1, step, width_a), lambda b, i: (b, i, 0))
    return pl.pallas_call(
        _attn_a_kernel,
        grid=grid,
        in_specs=[
            q_block,
            pl.BlockSpec((1, S, width_a), lambda b, i: (b, 0, 0)),
            pl.BlockSpec((1, S // LANES, width_a, LANES), lambda b, i: (b, 0, 0, 0)),
            q_block,
            pl.BlockSpec((1,) + bias_rows.shape[1:], lambda b, i: (layer, 0, 0, 0)),
        ],
        out_specs=q_block,
        out_shape=jax.ShapeDtypeStruct((B, S, width_a), jnp.bfloat16),
        scratch_shapes=[
            pltpu.VMEM((width_a // LANES, A_BIAS_ROWS, 2 * A_Q), jnp.float32),
            pltpu.VMEM((2, A_WIN, 2 * A_Q), jnp.float32),
            pltpu.VMEM((2, A_WIN, 2 * A_Q), jnp.bfloat16),
        ],
        compiler_params=pltpu.CompilerParams(
            dimension_semantics=("arbitrary", "arbitrary"), vmem_limit_bytes=VMEM_LIMIT_BYTES),
    )(q_all, k_all, vt_all, g_all, bias_rows)


def _bias_rows(rel_bias):
    n_blk = A_BIAS_ROWS // A_Q
    reach = A_BIAS_ROWS
    table = jnp.pad(rel_bias.astype(jnp.float32) * LOG2E, ((0, 0), (0, 0), (reach, reach)), mode="edge")
    centre = reach + REL_CLIP
    rows = []
    for a in range(n_blk):
        rel0 = centre + (A_LEFT_BLOCKS - a) * A_Q
        rows.append(jnp.concatenate([table[..., rel0:rel0 + A_Q], table[..., rel0 - A_Q:rel0]], axis=-1))
    return jnp.stack(rows, axis=2)


def _fill_bias_table(f_ref, bias_ref):
    _, n_heads, n_blk, _ = f_ref.shape
    row = lax.broadcasted_iota(jnp.int32, (A_Q, A_Q), 0)
    qchunk = lax.broadcasted_iota(jnp.int32, (A_Q, A_Q), 1) // CHUNK
    for h in range(n_heads):
        pair, half = divmod(h, 2)
        for a in range(n_blk):
            f = jnp.broadcast_to(f_ref[0, h, a:a + 1, :], (A_Q, 2 * A_Q))
            t = pltpu.roll(f, 0, 1, stride=1, stride_axis=0)[:, :A_Q]
            dchunk = qchunk - (row + a * A_Q) // CHUNK + N_LEFT_CHUNKS
            valid = (dchunk >= 0) & (dchunk <= N_LEFT_CHUNKS)
            bias_ref[pair, a * A_Q:(a + 1) * A_Q, half * A_Q:(half + 1) * A_Q] = jnp.where(valid, t, NEG_INF)


def _attn_b_kernel(q_ref, k_ref, vt_ref, g_ref, lq1_ref, lk1_ref, lq2_ref, lk2_ref, sub_ref,
                   o_ref, qm_ref, s_ref, bm_ref, m_ref, acc_ref, *, lam_init):
    n_q = q_ref.shape[1] // B_Q
    heads = range(B_HEADS)
    lane = lax.broadcasted_iota(jnp.int32, (B_Q, LANES), 1)
    first = lane < HEAD_DIM
    ones = jnp.ones((BF16_SUBLANES, B_K), jnp.bfloat16)
    blocks_per_step = B_K // LANES
    lam = (jnp.exp(jnp.sum(lq1_ref[0] * lk1_ref[0], axis=1, keepdims=True))
           - jnp.exp(jnp.sum(lq2_ref[0] * lk2_ref[0], axis=1, keepdims=True)) + lam_init)

    def head_lanes(t):
        return slice(t * LANES, (t + 1) * LANES)

    def scores(j, slot):
        for t in heads:
            kj = k_ref[0, pl.ds(pl.multiple_of(j * B_K, B_K), B_K), head_lanes(t)]
            for n in range(2):
                s = lax.dot_general(kj, qm_ref[t, n], _NT, preferred_element_type=jnp.float32)
                s_ref[t, slot, n] = s
                bm_ref[t, slot, n] = jnp.max(s, axis=0, keepdims=True)

    def accumulate(j, slot, masked=False):
        for t in heads:
            vj = jnp.concatenate([vt_ref[0, j * blocks_per_step + u, head_lanes(t), :]
                                  for u in range(blocks_per_step)], axis=1)
            vj = jnp.concatenate([vj, ones], axis=0)
            for n in range(2):
                s = s_ref[t, slot, n]
                if masked:
                    kc = lax.broadcasted_iota(jnp.int32, (B_K, B_Q), 0) // CHUNK
                    qc = lax.broadcasted_iota(jnp.int32, (B_K, B_Q), 1) // CHUNK
                    s = jnp.where(kc <= qc, s, NEG_INF)
                    block_max = jnp.max(s, axis=0, keepdims=True)
                else:
                    block_max = bm_ref[t, slot, n]
                m_old = m_ref[t, n]
                m_new = jnp.maximum(m_old, block_max)
                alpha = jnp.exp2(m_old - m_new)
                p = jnp.exp2(s - m_new).astype(jnp.bfloat16)
                acc_ref[t, n] = alpha * acc_ref[t, n] + jnp.dot(vj, p, preferred_element_type=jnp.float32)
                m_ref[t, n] = m_new

    def two_blocks(jj, carry):
        j = 2 * jj
        scores(j + 1, 1)
        accumulate(j, 0)
        scores(j + 2, 0)
        accumulate(j + 1, 1)
        return carry

    def first_scores(i):
        rows = pl.ds(pl.multiple_of(i * B_Q, B_Q), B_Q)
        for t in heads:
            qp = q_ref[0, rows, head_lanes(t)]
            zero = jnp.zeros_like(qp)
            qm_ref[t, 0] = jnp.where(first, qp, zero)
            qm_ref[t, 1] = jnp.where(first, zero, qp)
        scores(0, 0)

    def query_block(i, carry):
        rows = pl.ds(pl.multiple_of(i * B_Q, B_Q), B_Q)
        acc_ref[...] = jnp.zeros_like(acc_ref)
        m_ref[...] = jnp.full(m_ref.shape, NEG_INF, jnp.float32)
        lax.fori_loop(0, i // 2, two_blocks, 0)

        @pl.when(i % 2 == 1)
        def _():
            scores(i, 1)
            accumulate(i - 1, 0)

        accumulate(i, i % 2, masked=True)
        first_scores(jnp.minimum(i + 1, n_q - 1))

        for t in heads:
            l0 = acc_ref[t, 0, LANES:LANES + 1, :]
            l1 = acc_ref[t, 1, LANES:LANES + 1, :]
            ot = (acc_ref[t, 0, :LANES, :] * (1.0 / l0)
                  - lam * (acc_ref[t, 1, :LANES, :] * (1.0 / l1)))
            ot = ot * lax.rsqrt(jnp.mean(ot * ot, axis=0, keepdims=True) + RMS_EPS)
            o = ot.T * (sub_ref[0] * (1.0 - lam_init))
            g = g_ref[0, rows, head_lanes(t)].astype(jnp.float32)
            o_ref[0, rows, head_lanes(t)] = (o * _silu(g)).astype(o_ref.dtype)
        return carry

    first_scores(0)
    lax.fori_loop(0, n_q, query_block, 0)


def _attn_b(q_all, k_all, vt_all, g_all, lq1, lk1, lq2, lk2, subln, width_a, lam_init, layer):
    B, S, width = q_all.shape
    step_lanes = B_HEADS * LANES
    col0 = width_a // step_lanes
    grid = (B, (width - width_a) // step_lanes)
    seq_block = pl.BlockSpec((1, S, step_lanes), lambda b, h: (b, 0, col0 + h))
    vec64 = pl.BlockSpec((1, 1, HEAD_DIM), lambda b, h: (layer, 0, 0))
    return pl.pallas_call(
        functools.partial(_attn_b_kernel, lam_init=lam_init),
        grid=grid,
        in_specs=[
            seq_block,
            seq_block,
            pl.BlockSpec((1, S // LANES, step_lanes, LANES), lambda b, h: (b, 0, col0 + h, 0)),
            seq_block,
            vec64, vec64, vec64, vec64,
            pl.BlockSpec((1, 1, LANES), lambda b, h: (layer, 0, 0)),
        ],
        out_specs=pl.BlockSpec((1, S, step_lanes), lambda b, h: (b, 0, h)),
        out_shape=jax.ShapeDtypeStruct((B, S, width - width_a), jnp.bfloat16),
        scratch_shapes=[
            pltpu.VMEM((B_HEADS, 2, B_Q, LANES), jnp.bfloat16),
            pltpu.VMEM((B_HEADS, 2, 2, B_K, B_Q), jnp.float32),
            pltpu.VMEM((B_HEADS, 2, 2, 1, B_Q), jnp.float32),
            pltpu.VMEM((B_HEADS, 2, 1, B_Q), jnp.float32),
            pltpu.VMEM((B_HEADS, 2, ACC_ROWS, B_Q), jnp.float32),
        ],
        compiler_params=pltpu.CompilerParams(
            dimension_semantics=("arbitrary", "arbitrary"), vmem_limit_bytes=VMEM_LIMIT_BYTES),
    )(q_all, k_all, vt_all, g_all, lq1, lk1, lq2, lk2, subln)


def _out_proj_kernel(x_ref, ya_ref, yb_ref, w_ref, g_ref, b_ref, o_ref, wb_ref, *, alpha):
    width_a = ya_ref.shape[2]

    @pl.when((pl.program_id(0) == 0) & (pl.program_id(1) == 0))
    def _():
        wb_ref[...] = w_ref[0].astype(jnp.bfloat16)

    for r in range(x_ref.shape[1] // OUT_SUB_ROWS):
        rows = slice(r * OUT_SUB_ROWS, (r + 1) * OUT_SUB_ROWS)
        y = jnp.dot(ya_ref[0, rows, :], wb_ref[:width_a, :], preferred_element_type=jnp.float32)
        y = y + jnp.dot(yb_ref[0, rows, :], wb_ref[width_a:, :], preferred_element_type=jnp.float32)
        z = alpha * x_ref[0, rows, :] + y
        mu = jnp.mean(z, axis=1, keepdims=True)
        zc = z - mu
        var = jnp.mean(zc * zc, axis=1, keepdims=True)
        o_ref[0, rows, :] = zc * lax.rsqrt(var + LN_EPS) * g_ref[0] + b_ref[0]


def _out_proj(x, ya, yb, w, ln_g, ln_b, alpha, layer):
    B, S, D = x.shape
    grid = (B, S // OUT_ROWS)
    x_block = pl.BlockSpec((1, OUT_ROWS, D), lambda b, i: (b, i, 0))
    vec = pl.BlockSpec((1, 1, D), lambda b, i: (layer, 0, 0))
    return pl.pallas_call(
        functools.partial(_out_proj_kernel, alpha=alpha),
        grid=grid,
        in_specs=[
            x_block,
            pl.BlockSpec((1, OUT_ROWS, ya.shape[2]), lambda b, i: (b, i, 0)),
            pl.BlockSpec((1, OUT_ROWS, yb.shape[2]), lambda b, i: (b, i, 0)),
            pl.BlockSpec((1,) + w.shape[1:], lambda b, i: (layer, 0, 0), pipeline_mode=pl.Buffered(1)),
            vec, vec,
        ],
        out_specs=x_block,
        out_shape=jax.ShapeDtypeStruct((B, S, D), jnp.float32),
        scratch_shapes=[pltpu.VMEM(w.shape[1:], jnp.bfloat16)],
        compiler_params=pltpu.CompilerParams(
            dimension_semantics=("arbitrary", "arbitrary"), vmem_limit_bytes=VMEM_LIMIT_BYTES),
    )(x, ya, yb, w, ln_g, ln_b)


def _rope_tables(seq):
    half = HEAD_DIM // 2
    inv_freq = ROPE_THETA ** (-jnp.arange(0, HEAD_DIM, 2, dtype=jnp.float32) / HEAD_DIM)
    pos = jnp.arange(seq, dtype=jnp.float32)
    ang = pos[:, None] * inv_freq[None, :]
    reps = LANES // half
    cos = jnp.concatenate([jnp.cos(ang)] * reps, axis=-1)
    sin = jnp.concatenate([jnp.sin(ang)] * reps, axis=-1)
    low = (np.arange(LANES) % HEAD_DIM) < half
    return cos, jnp.where(low, -sin, 0.0), jnp.where(low, 0.0, sin)


def kernel(x, w_in, w_out, rel_bias, lambda_q1, lambda_k1, lambda_q2, lambda_k2, subln_g, ln_g, ln_b):
    depth, d_model, proj_cols = w_in.shape
    seq = x.shape[1]
    width_a = proj_cols // 8
    alpha = (2 * depth) ** 0.25
    scale = HEAD_DIM ** -0.5
    cos, sina, sinb = _rope_tables(seq)
    q_scale = scale * LOG2E
    bias_rows = _bias_rows(rel_bias)
    lq1, lk1, lq2, lk2, subln, g_ln, b_ln = (
        p.astype(jnp.float32)[:, None, :] for p in (lambda_q1, lambda_k1, lambda_q2, lambda_k2, subln_g, ln_g, ln_b))
    for l in range(depth):
        q_all, k_all, g_all, vt_all = _in_proj(x, w_in, cos, sina, sinb, width_a, q_scale, l)
        ya = _attn_a(q_all, k_all, vt_all, g_all, bias_rows, width_a, l)
        lam_init = 0.8 - 0.6 * math.exp(-0.3 * l)
        yb = _attn_b(q_all, k_all, vt_all, g_all, lq1, lk1, lq2, lk2, subln, width_a, lam_init, l)
        x = _out_proj(x, ya, yb, w_out, g_ln, b_ln, alpha, l)
    return x
```

```python
import functools
import math

import numpy as np
import jax
import jax.numpy as jnp
from jax import lax
from jax.experimental import pallas as pl
from jax.experimental.pallas import tpu as pltpu

CHUNK = 64
N_LEFT_CHUNKS = 8
HEAD_DIM = 64
REL_CLIP = 128
ROPE_THETA = 10000.0
LN_EPS = 1e-5
RMS_EPS = 1e-5
NEG_INF = -1e30

LANES = 128
VMEM_LIMIT_BYTES = 52 * 1024 * 1024

PROJ_ROWS = 512
PROJ_COLS_CHUNK = 512
OUT_ROWS = 2048
OUT_SUB_ROWS = 256
A_Q = 128
A_WIN = A_Q + N_LEFT_CHUNKS * CHUNK
A_LEFT_BLOCKS = N_LEFT_CHUNKS * CHUNK // A_Q
A_BIAS_ROWS = A_WIN + A_LEFT_BLOCKS * A_Q
A_STEP_BLOCKS = 8
B_Q = 512
B_K = 512
B_HEADS = 2
BF16_SUBLANES = 16
ACC_ROWS = LANES + BF16_SUBLANES
LOG2E = 1.4426950408889634

_NT = (((1,), (1,)), ((), ()))


def _silu(g):
    return g * (1.0 / (1.0 + jnp.exp(-g)))


_STD_BLOCKS = (0, 4, 1, 5, 3, 7)
_V_BLOCKS = (2, 6)


def _in_proj_kernel(x_ref, w_ref, cos_ref, sina_ref, sinb_ref,
                    q_ref, k_ref, g_ref, vt_ref, wstd_ref, wvt_ref, *, width_a, q_scale):
    @pl.when((pl.program_id(0) == 0) & (pl.program_id(1) == 0))
    def _():
        for c, blk in enumerate(_STD_BLOCKS):
            w = w_ref[0, :, blk * width_a:(blk + 1) * width_a]
            if c < 2:
                w = w * q_scale
            wstd_ref[:, c * width_a:(c + 1) * width_a] = w.astype(jnp.bfloat16)
        for c, blk in enumerate(_V_BLOCKS):
            for t in range(width_a // LANES):
                src = slice(blk * width_a + t * LANES, blk * width_a + (t + 1) * LANES)
                dst = slice(c * width_a + t * LANES, c * width_a + (t + 1) * LANES)
                wvt_ref[dst, :] = w_ref[0, :, src].T.astype(jnp.bfloat16)

    xb = x_ref[0].astype(jnp.bfloat16)
    n_std = wstd_ref.shape[1] // PROJ_COLS_CHUNK
    per_kind = n_std // 3
    outs = (q_ref, k_ref, g_ref)
    cos = cos_ref[...]
    sina = sina_ref[...]
    sinb = sinb_ref[...]
    for c in range(n_std):
        kind, sub = divmod(c, per_kind)
        col0 = sub * PROJ_COLS_CHUNK
        r = jnp.dot(xb, wstd_ref[:, c * PROJ_COLS_CHUNK:(c + 1) * PROJ_COLS_CHUNK],
                    preferred_element_type=jnp.float32)
        rotary = kind < 2 and col0 >= width_a
        for t in range(PROJ_COLS_CHUNK // LANES):
            rt = r[:, t * LANES:(t + 1) * LANES]
            if rotary:
                rt = (rt * cos + pltpu.roll(rt, LANES - HEAD_DIM // 2, 1) * sina
                      + pltpu.roll(rt, HEAD_DIM // 2, 1) * sinb)
            outs[kind][0, :, col0 + t * LANES:col0 + (t + 1) * LANES] = rt.astype(jnp.bfloat16)
    rows = xb.shape[0]
    n_v = wvt_ref.shape[0] // PROJ_COLS_CHUNK
    for c in range(n_v):
        rt = lax.dot_general(wvt_ref[c * PROJ_COLS_CHUNK:(c + 1) * PROJ_COLS_CHUNK, :], xb, _NT,
                             preferred_element_type=jnp.float32)
        for j in range(rows // LANES):
            vt_ref[0, j, c * PROJ_COLS_CHUNK:(c + 1) * PROJ_COLS_CHUNK, :] = (
                rt[:, j * LANES:(j + 1) * LANES].astype(jnp.bfloat16))


def _in_proj(x, w_in, cos, sina, sinb, width_a, q_scale, layer):
    B, S, D = x.shape
    proj_cols = w_in.shape[2]
    width = len(_V_BLOCKS) * width_a
    grid = (B, S // PROJ_ROWS)
    row_block = pl.BlockSpec((1, PROJ_ROWS, width), lambda b, i: (b, i, 0))
    tab_block = pl.BlockSpec((PROJ_ROWS, LANES), lambda b, i: (i, 0))
    return pl.pallas_call(
        functools.partial(_in_proj_kernel, width_a=width_a, q_scale=q_scale),
        grid=grid,
        in_specs=[
            pl.BlockSpec((1, PROJ_ROWS, D), lambda b, i: (b, i, 0)),
            pl.BlockSpec((1, D, proj_cols), lambda b, i: (layer, 0, 0), pipeline_mode=pl.Buffered(1)),
            tab_block, tab_block, tab_block,
        ],
        out_specs=[
            row_block, row_block, row_block,
            pl.BlockSpec((1, PROJ_ROWS // LANES, width, LANES), lambda b, i: (b, i, 0, 0)),
        ],
        out_shape=[
            jax.ShapeDtypeStruct((B, S, width), jnp.bfloat16),
            jax.ShapeDtypeStruct((B, S, width), jnp.bfloat16),
            jax.ShapeDtypeStruct((B, S, width), jnp.bfloat16),
            jax.ShapeDtypeStruct((B, S // LANES, width, LANES), jnp.bfloat16),
        ],
        scratch_shapes=[
            pltpu.VMEM((D, len(_STD_BLOCKS) * width_a), jnp.bfloat16),
            pltpu.VMEM((width, D), jnp.bfloat16),
        ],
        compiler_params=pltpu.CompilerParams(
            dimension_semantics=("arbitrary", "arbitrary"), vmem_limit_bytes=VMEM_LIMIT_BYTES),
    )(x, w_in, cos, sina, sinb)


def _attn_a_kernel(q_ref, k_ref, vt_ref, g_ref, f_ref, o_ref, bias_ref, s_ref, p_ref):
    i = pl.program_id(1)
    n_pairs = q_ref.shape[2] // LANES

    @pl.when((pl.program_id(0) == 0) & (i == 0))
    def _():
        _fill_bias_table(f_ref, bias_ref)

    lane = lax.broadcasted_iota(jnp.int32, (A_Q, LANES), 1)
    first = lane < HEAD_DIM
    ones = jnp.ones((BF16_SUBLANES, A_WIN), jnp.bfloat16)
    items = [(sb, p) for sb in range(A_STEP_BLOCKS) for p in range(n_pairs)]

    def first_key_block(sb):
        return jnp.maximum(i * A_STEP_BLOCKS + sb - A_LEFT_BLOCKS, 0)

    def scores(t):
        sb, p = items[t]
        cols = slice(p * LANES, (p + 1) * LANES)
        qp = q_ref[0, sb * A_Q:(sb + 1) * A_Q, cols]
        zero = jnp.zeros_like(qp)
        q2 = jnp.concatenate([jnp.where(first, qp, zero), jnp.where(first, zero, qp)], axis=0)
        start = pl.multiple_of(first_key_block(sb) * A_Q, A_Q)
        kw = k_ref[0, pl.ds(start, A_WIN), cols]
        s_ref[t % 2] = lax.dot_general(kw, q2, _NT, preferred_element_type=jnp.float32)

    def softmax(t):
        sb, p = items[t]
        blk0 = first_key_block(sb)
        bias_row = pl.multiple_of((A_LEFT_BLOCKS - (i * A_STEP_BLOCKS + sb - blk0)) * A_Q, A_Q)
        s = s_ref[t % 2] + bias_ref[p, pl.ds(bias_row, A_WIN), :]
        m = jnp.max(s, axis=0, keepdims=True)
        p_ref[t % 2] = jnp.exp2(s - m).astype(jnp.bfloat16)

    def finish(t):
        sb, p = items[t]
        cols = slice(p * LANES, (p + 1) * LANES)
        rows = slice(sb * A_Q, (sb + 1) * A_Q)
        blk0 = first_key_block(sb)
        vw = jnp.concatenate([vt_ref[0, blk0 + u, cols, :] for u in range(A_WIN // LANES)], axis=1)
        vw = jnp.concatenate([vw, ones], axis=0)
        r = jnp.dot(vw, p_ref[t % 2], preferred_element_type=jnp.float32)
        r = r[:LANES] * (1.0 / r[LANES:LANES + 1])
        ot = jnp.concatenate([r[:HEAD_DIM, :A_Q], r[HEAD_DIM:, A_Q:]], axis=0)
        g = g_ref[0, rows, cols].astype(jnp.float32)
        o_ref[0, rows, cols] = (ot.T * _silu(g)).astype(o_ref.dtype)

    n_items = len(items)
    scores(0)
    scores(1)
    softmax(0)
    for t in range(n_items):
        if t + 2 < n_items:
            scores(t + 2)
        if t + 1 < n_items:
            softmax(t + 1)
        finish(t)


def _attn_a(q_all, k_all, vt_all, g_all, bias_rows, width_a, layer):
    B, S, _ = q_all.shape
    step = A_STEP_BLOCKS * A_Q
    grid = (B, S // step)
    q_block = pl.BlockSpec((1, step, width_a), lambda b, i: (b, i, 0))
    return pl.pallas_call(
        _attn_a_kernel,
        grid=grid,
        in_specs=[
            q_block,
            pl.BlockSpec((1, S, width_a), lambda b, i: (b, 0, 0)),
            pl.BlockSpec((1, S // LANES, width_a, LANES), lambda b, i: (b, 0, 0, 0)),
            q_block,
            pl.BlockSpec((1,) + bias_rows.shape[1:], lambda b, i: (layer, 0, 0, 0)),
        ],
        out_specs=q_block,
        out_shape=jax.ShapeDtypeStruct((B, S, width_a), jnp.bfloat16),
        scratch_shapes=[
            pltpu.VMEM((width_a // LANES, A_BIAS_ROWS, 2 * A_Q), jnp.float32),
            pltpu.VMEM((2, A_WIN, 2 * A_Q), jnp.float32),
            pltpu.VMEM((2, A_WIN, 2 * A_Q), jnp.bfloat16),
        ],
        compiler_params=pltpu.CompilerParams(
            dimension_semantics=("arbitrary", "arbitrary"), vmem_limit_bytes=VMEM_LIMIT_BYTES),
    )(q_all, k_all, vt_all, g_all, bias_rows)


def _bias_rows(rel_bias):
    n_blk = A_BIAS_ROWS // A_Q
    reach = A_BIAS_ROWS
    table = jnp.pad(rel_bias.astype(jnp.float32) * LOG2E, ((0, 0), (0, 0), (reach, reach)), mode="edge")
    centre = reach + REL_CLIP
    rows = []
    for a in range(n_blk):
        rel0 = centre + (A_LEFT_BLOCKS - a) * A_Q
        rows.append(jnp.concatenate([table[..., rel0:rel0 + A_Q], table[..., rel0 - A_Q:rel0]], axis=-1))
    return jnp.stack(rows, axis=2)


def _fill_bias_table(f_ref, bias_ref):
    _, n_heads, n_blk, _ = f_ref.shape
    row = lax.broadcasted_iota(jnp.int32, (A_Q, A_Q), 0)
    qchunk = lax.broadcasted_iota(jnp.int32, (A_Q, A_Q), 1) // CHUNK
    for h in range(n_heads):
        pair, half = divmod(h, 2)
        for a in range(n_blk):
            f = jnp.broadcast_to(f_ref[0, h, a:a + 1, :], (A_Q, 2 * A_Q))
            t = pltpu.roll(f, 0, 1, stride=1, stride_axis=0)[:, :A_Q]
            dchunk = qchunk - (row + a * A_Q) // CHUNK + N_LEFT_CHUNKS
            valid = (dchunk >= 0) & (dchunk <= N_LEFT_CHUNKS)
            bias_ref[pair, a * A_Q:(a + 1) * A_Q, half * A_Q:(half + 1) * A_Q] = jnp.where(valid, t, NEG_INF)


def _attn_b_kernel(q_ref, k_ref, vt_ref, g_ref, lq1_ref, lk1_ref, lq2_ref, lk2_ref, sub_ref,
                   o_ref, qm_ref, s_ref, bm_ref, m_ref, acc_ref, *, lam_init):
    n_q = q_ref.shape[1] // B_Q
    heads = range(B_HEADS)
    lane = lax.broadcasted_iota(jnp.int32, (B_Q, LANES), 1)
    first = lane < HEAD_DIM
    ones = jnp.ones((BF16_SUBLANES, B_K), jnp.bfloat16)
    blocks_per_step = B_K // LANES
    lam = (jnp.exp(jnp.sum(lq1_ref[0] * lk1_ref[0], axis=1, keepdims=True))
           - jnp.exp(jnp.sum(lq2_ref[0] * lk2_ref[0], axis=1, keepdims=True)) + lam_init)

    def head_lanes(t):
        return slice(t * LANES, (t + 1) * LANES)

    def scores(j, slot, which=heads):
        for t in which:
            kj = k_ref[0, pl.ds(pl.multiple_of(j * B_K, B_K), B_K), head_lanes(t)]
            for n in range(2):
                s = lax.dot_general(kj, qm_ref[t, n], _NT, preferred_element_type=jnp.float32)
                s_ref[t, slot, n] = s
                bm_ref[t, slot, n] = jnp.max(s, axis=0, keepdims=True)

    def accumulate(j, slot, masked=False, which=heads):
        for t in which:
            vj = jnp.concatenate([vt_ref[0, j * blocks_per_step + u, head_lanes(t), :]
                                  for u in range(blocks_per_step)], axis=1)
            vj = jnp.concatenate([vj, ones], axis=0)
            for n in range(2):
                s = s_ref[t, slot, n]
                if masked:
                    kc = lax.broadcasted_iota(jnp.int32, (B_K, B_Q), 0) // CHUNK
                    qc = lax.broadcasted_iota(jnp.int32, (B_K, B_Q), 1) // CHUNK
                    s = jnp.where(kc <= qc, s, NEG_INF)
                    block_max = jnp.max(s, axis=0, keepdims=True)
                else:
                    block_max = bm_ref[t, slot, n]
                m_old = m_ref[t, n]
                m_new = jnp.maximum(m_old, block_max)
                alpha = jnp.exp2(m_old - m_new)
                p = jnp.exp2(s - m_new).astype(jnp.bfloat16)
                acc_ref[t, n] = alpha * acc_ref[t, n] + jnp.dot(vj, p, preferred_element_type=jnp.float32)
                m_ref[t, n] = m_new

    def two_blocks(jj, carry):
        j = 2 * jj
        for t in heads:
            scores(j + 1, 1, which=(t,))
            accumulate(j, 0, which=(t,))
        for t in heads:
            scores(j + 2, 0, which=(t,))
            accumulate(j + 1, 1, which=(t,))
        return carry

    def first_scores(i):
        rows = pl.ds(pl.multiple_of(i * B_Q, B_Q), B_Q)
        for t in heads:
            qp = q_ref[0, rows, head_lanes(t)]
            zero = jnp.zeros_like(qp)
            qm_ref[t, 0] = jnp.where(first, qp, zero)
            qm_ref[t, 1] = jnp.where(first, zero, qp)
        scores(0, 0)

    def query_block(i, carry):
        rows = pl.ds(pl.multiple_of(i * B_Q, B_Q), B_Q)
        acc_ref[...] = jnp.zeros_like(acc_ref)
        m_ref[...] = jnp.full(m_ref.shape, NEG_INF, jnp.float32)
        lax.fori_loop(0, i // 2, two_blocks, 0)

        @pl.when(i % 2 == 1)
        def _():
            scores(i, 1)
            accumulate(i - 1, 0)

        accumulate(i, i % 2, masked=True)
        first_scores(jnp.minimum(i + 1, n_q - 1))

        for t in heads:
            l0 = acc_ref[t, 0, LANES:LANES + 1, :]
            l1 = acc_ref[t, 1, LANES:LANES + 1, :]
            ot = (acc_ref[t, 0, :LANES, :] * (1.0 / l0)
                  - lam * (acc_ref[t, 1, :LANES, :] * (1.0 / l1)))
            ot = ot * lax.rsqrt(jnp.mean(ot * ot, axis=0, keepdims=True) + RMS_EPS)
            o = ot.T * (sub_ref[0] * (1.0 - lam_init))
            g = g_ref[0, rows, head_lanes(t)].astype(jnp.float32)
            o_ref[0, rows, head_lanes(t)] = (o * _silu(g)).astype(o_ref.dtype)
        return carry

    first_scores(0)
    lax.fori_loop(0, n_q, query_block, 0)


def _attn_b(q_all, k_all, vt_all, g_all, lq1, lk1, lq2, lk2, subln, width_a, lam_init, layer):
    B, S, width = q_all.shape
    step_lanes = B_HEADS * LANES
    col0 = width_a // step_lanes
    grid = (B, (width - width_a) // step_lanes)
    seq_block = pl.BlockSpec((1, S, step_lanes), lambda b, h: (b, 0, col0 + h))
    vec64 = pl.BlockSpec((1, 1, HEAD_DIM), lambda b, h: (layer, 0, 0))
    return pl.pallas_call(
        functools.partial(_attn_b_kernel, lam_init=lam_init),
        grid=grid,
        in_specs=[
            seq_block,
            seq_block,
            pl.BlockSpec((1, S // LANES, step_lanes, LANES), lambda b, h: (b, 0, col0 + h, 0)),
            seq_block,
            vec64, vec64, vec64, vec64,
            pl.BlockSpec((1, 1, LANES), lambda b, h: (layer, 0, 0)),
        ],
        out_specs=pl.BlockSpec((1, S, step_lanes), lambda b, h: (b, 0, h)),
        out_shape=jax.ShapeDtypeStruct((B, S, width - width_a), jnp.bfloat16),
        scratch_shapes=[
            pltpu.VMEM((B_HEADS, 2, B_Q, LANES), jnp.bfloat16),
            pltpu.VMEM((B_HEADS, 2, 2, B_K, B_Q), jnp.float32),
            pltpu.VMEM((B_HEADS, 2, 2, 1, B_Q), jnp.float32),
            pltpu.VMEM((B_HEADS, 2, 1, B_Q), jnp.float32),
            pltpu.VMEM((B_HEADS, 2, ACC_ROWS, B_Q), jnp.float32),
        ],
        compiler_params=pltpu.CompilerParams(
            dimension_semantics=("arbitrary", "arbitrary"), vmem_limit_bytes=VMEM_LIMIT_BYTES),
    )(q_all, k_all, vt_all, g_all, lq1, lk1, lq2, lk2, subln)


def _out_proj_kernel(x_ref, ya_ref, yb_ref, w_ref, g_ref, b_ref, o_ref, wb_ref, *, alpha):
    width_a = ya_ref.shape[2]

    @pl.when((pl.program_id(0) == 0) & (pl.program_id(1) == 0))
    def _():
        wb_ref[...] = w_ref[0].astype(jnp.bfloat16)

    for r in range(x_ref.shape[1] // OUT_SUB_ROWS):
        rows = slice(r * OUT_SUB_ROWS, (r + 1) * OUT_SUB_ROWS)
        y = jnp.dot(ya_ref[0, rows, :], wb_ref[:width_a, :], preferred_element_type=jnp.float32)
        y = y + jnp.dot(yb_ref[0, rows, :], wb_ref[width_a:, :], preferred_element_type=jnp.float32)
        z = alpha * x_ref[0, rows, :] + y
        mu = jnp.mean(z, axis=1, keepdims=True)
        zc = z - mu
        var = jnp.mean(zc * zc, axis=1, keepdims=True)
        o_ref[0, rows, :] = zc * lax.rsqrt(var + LN_EPS) * g_ref[0] + b_ref[0]


def _out_proj(x, ya, yb, w, ln_g, ln_b, alpha, layer):
    B, S, D = x.shape
    grid = (B, S // OUT_ROWS)
    x_block = pl.BlockSpec((1, OUT_ROWS, D), lambda b, i: (b, i, 0))
    vec = pl.BlockSpec((1, 1, D), lambda b, i: (layer, 0, 0))
    return pl.pallas_call(
        functools.partial(_out_proj_kernel, alpha=alpha),
        grid=grid,
        in_specs=[
            x_block,
            pl.BlockSpec((1, OUT_ROWS, ya.shape[2]), lambda b, i: (b, i, 0)),
            pl.BlockSpec((1, OUT_ROWS, yb.shape[2]), lambda b, i: (b, i, 0)),
            pl.BlockSpec((1,) + w.shape[1:], lambda b, i: (layer, 0, 0), pipeline_mode=pl.Buffered(1)),
            vec, vec,
        ],
        out_specs=x_block,
        out_shape=jax.ShapeDtypeStruct((B, S, D), jnp.float32),
        scratch_shapes=[pltpu.VMEM(w.shape[1:], jnp.bfloat16)],
        compiler_params=pltpu.CompilerParams(
            dimension_semantics=("arbitrary", "arbitrary"), vmem_limit_bytes=VMEM_LIMIT_BYTES),
    )(x, ya, yb, w, ln_g, ln_b)


def _rope_tables(seq):
    half = HEAD_DIM // 2
    inv_freq = ROPE_THETA ** (-jnp.arange(0, HEAD_DIM, 2, dtype=jnp.float32) / HEAD_DIM)
    pos = jnp.arange(seq, dtype=jnp.float32)
    ang = pos[:, None] * inv_freq[None, :]
    reps = LANES // half
    cos = jnp.concatenate([jnp.cos(ang)] * reps, axis=-1)
    sin = jnp.concatenate([jnp.sin(ang)] * reps, axis=-1)
    low = (np.arange(LANES) % HEAD_DIM) < half
    return cos, jnp.where(low, -sin, 0.0), jnp.where(low, 0.0, sin)


def kernel(x, w_in, w_out, rel_bias, lambda_q1, lambda_k1, lambda_q2, lambda_k2, subln_g, ln_g, ln_b):
    depth, d_model, proj_cols = w_in.shape
    seq = x.shape[1]
    width_a = proj_cols // 8
    alpha = (2 * depth) ** 0.25
    scale = HEAD_DIM ** -0.5
    cos, sina, sinb = _rope_tables(seq)
    q_scale = scale * LOG2E
    bias_rows = _bias_rows(rel_bias)
    lq1, lk1, lq2, lk2, subln, g_ln, b_ln = (
        p.astype(jnp.float32)[:, None, :] for p in (lambda_q1, lambda_k1, lambda_q2, lambda_k2, subln_g, ln_g, ln_b))
    for l in range(depth):
        q_all, k_all, g_all, vt_all = _in_proj(x, w_in, cos, sina, sinb, width_a, q_scale, l)
        ya = _attn_a(q_all, k_all, vt_all, g_all, bias_rows, width_a, l)
        lam_init = 0.8 - 0.6 * math.exp(-0.3 * l)
        yb = _attn_b(q_all, k_all, vt_all, g_all, lq1, lk1, lq2, lk2, subln, width_a, lam_init, l)
        x = _out_proj(x, ya, yb, w_out, g_ln, b_ln, alpha, l)
    return x
```

```python
import functools
import math

import numpy as np
import jax
import jax.numpy as jnp
from jax import lax
from jax.experimental import pallas as pl
from jax.experimental.pallas import tpu as pltpu

CHUNK = 64
N_LEFT_CHUNKS = 8
HEAD_DIM = 64
REL_CLIP = 128
ROPE_THETA = 10000.0
LN_EPS = 1e-5
RMS_EPS = 1e-5
NEG_INF = -1e30

LANES = 128
VMEM_LIMIT_BYTES = 52 * 1024 * 1024

PROJ_ROWS = 512
PROJ_COLS_CHUNK = 512
OUT_ROWS = 2048
OUT_SUB_ROWS = 256
A_Q = 128
A_WIN = A_Q + N_LEFT_CHUNKS * CHUNK
A_LEFT_BLOCKS = N_LEFT_CHUNKS * CHUNK // A_Q
A_BIAS_ROWS = A_WIN + A_LEFT_BLOCKS * A_Q
A_STEP_BLOCKS = 8
B_Q = 512
B_K = 512
B_HEADS = 2
BF16_SUBLANES = 16
ACC_ROWS = LANES + BF16_SUBLANES
LOG2E = 1.4426950408889634

_NT = (((1,), (1,)), ((), ()))


def _silu(g):
    return g * (1.0 / (1.0 + jnp.exp(-g)))


_STD_BLOCKS = (0, 4, 1, 5, 3, 7)
_V_BLOCKS = (2, 6)


def _in_proj_kernel(x_ref, w_ref, cos_ref, sina_ref, sinb_ref,
                    q_ref, k_ref, g_ref, vt_ref, wstd_ref, wvt_ref, *, width_a, q_scale):
    @pl.when((pl.program_id(0) == 0) & (pl.program_id(1) == 0))
    def _():
        for c, blk in enumerate(_STD_BLOCKS):
            w = w_ref[0, :, blk * width_a:(blk + 1) * width_a]
            if c < 2:
                w = w * q_scale
            wstd_ref[:, c * width_a:(c + 1) * width_a] = w.astype(jnp.bfloat16)
        for c, blk in enumerate(_V_BLOCKS):
            for t in range(width_a // LANES):
                src = slice(blk * width_a + t * LANES, blk * width_a + (t + 1) * LANES)
                dst = slice(c * width_a + t * LANES, c * width_a + (t + 1) * LANES)
                wvt_ref[dst, :] = w_ref[0, :, src].T.astype(jnp.bfloat16)

    xb = x_ref[0].astype(jnp.bfloat16)
    n_std = wstd_ref.shape[1] // PROJ_COLS_CHUNK
    per_kind = n_std // 3
    outs = (q_ref, k_ref, g_ref)
    cos = cos_ref[...]
    sina = sina_ref[...]
    sinb = sinb_ref[...]
    for c in range(n_std):
        kind, sub = divmod(c, per_kind)
        col0 = sub * PROJ_COLS_CHUNK
        r = jnp.dot(xb, wstd_ref[:, c * PROJ_COLS_CHUNK:(c + 1) * PROJ_COLS_CHUNK],
                    preferred_element_type=jnp.float32)
        rotary = kind < 2 and col0 >= width_a
        for t in range(PROJ_COLS_CHUNK // LANES):
            rt = r[:, t * LANES:(t + 1) * LANES]
            if rotary:
                rt = (rt * cos + pltpu.roll(rt, LANES - HEAD_DIM // 2, 1) * sina
                      + pltpu.roll(rt, HEAD_DIM // 2, 1) * sinb)
            outs[kind][0, :, col0 + t * LANES:col0 + (t + 1) * LANES] = rt.astype(jnp.bfloat16)
    rows = xb.shape[0]
    n_v = wvt_ref.shape[0] // PROJ_COLS_CHUNK
    for c in range(n_v):
        rt = lax.dot_general(wvt_ref[c * PROJ_COLS_CHUNK:(c + 1) * PROJ_COLS_CHUNK, :], xb, _NT,
                             preferred_element_type=jnp.float32)
        for j in range(rows // LANES):
            vt_ref[0, j, c * PROJ_COLS_CHUNK:(c + 1) * PROJ_COLS_CHUNK, :] = (
                rt[:, j * LANES:(j + 1) * LANES].astype(jnp.bfloat16))


def _in_proj(x, w_in, cos, sina, sinb, width_a, q_scale, layer):
    B, S, D = x.shape
    proj_cols = w_in.shape[2]
    width = len(_V_BLOCKS) * width_a
    grid = (B, S // PROJ_ROWS)
    row_block = pl.BlockSpec((1, PROJ_ROWS, width), lambda b, i: (b, i, 0))
    tab_block = pl.BlockSpec((PROJ_ROWS, LANES), lambda b, i: (i, 0))
    return pl.pallas_call(
        functools.partial(_in_proj_kernel, width_a=width_a, q_scale=q_scale),
        grid=grid,
        in_specs=[
            pl.BlockSpec((1, PROJ_ROWS, D), lambda b, i: (b, i, 0)),
            pl.BlockSpec((1, D, proj_cols), lambda b, i: (layer, 0, 0), pipeline_mode=pl.Buffered(1)),
            tab_block, tab_block, tab_block,
        ],
        out_specs=[
            row_block, row_block, row_block,
            pl.BlockSpec((1, PROJ_ROWS // LANES, width, LANES), lambda b, i: (b, i, 0, 0)),
        ],
        out_shape=[
            jax.ShapeDtypeStruct((B, S, width), jnp.bfloat16),
            jax.ShapeDtypeStruct((B, S, width), jnp.bfloat16),
            jax.ShapeDtypeStruct((B, S, width), jnp.bfloat16),
            jax.ShapeDtypeStruct((B, S // LANES, width, LANES), jnp.bfloat16),
        ],
        scratch_shapes=[
            pltpu.VMEM((D, len(_STD_BLOCKS) * width_a), jnp.bfloat16),
            pltpu.VMEM((width, D), jnp.bfloat16),
        ],
        compiler_params=pltpu.CompilerParams(
            dimension_semantics=("arbitrary", "arbitrary"), vmem_limit_bytes=VMEM_LIMIT_BYTES),
    )(x, w_in, cos, sina, sinb)


def _attn_a_kernel(q_ref, k_ref, vt_ref, g_ref, f_ref, o_ref, bias_ref, s_ref, p_ref):
    i = pl.program_id(1)
    n_pairs = q_ref.shape[2] // LANES

    @pl.when((pl.program_id(0) == 0) & (i == 0))
    def _():
        _fill_bias_table(f_ref, bias_ref)

    lane = lax.broadcasted_iota(jnp.int32, (A_Q, LANES), 1)
    first = lane < HEAD_DIM
    ones = jnp.ones((BF16_SUBLANES, A_WIN), jnp.bfloat16)
    items = [(sb, p) for sb in range(A_STEP_BLOCKS) for p in range(n_pairs)]

    def first_key_block(sb):
        return jnp.maximum(i * A_STEP_BLOCKS + sb - A_LEFT_BLOCKS, 0)

    def scores(t):
        sb, p = items[t]
        cols = slice(p * LANES, (p + 1) * LANES)
        qp = q_ref[0, sb * A_Q:(sb + 1) * A_Q, cols]
        zero = jnp.zeros_like(qp)
        q2 = jnp.concatenate([jnp.where(first, qp, zero), jnp.where(first, zero, qp)], axis=0)
        start = pl.multiple_of(first_key_block(sb) * A_Q, A_Q)
        kw = k_ref[0, pl.ds(start, A_WIN), cols]
        s_ref[t % 2] = lax.dot_general(kw, q2, _NT, preferred_element_type=jnp.float32)

    def softmax(t):
        sb, p = items[t]
        blk0 = first_key_block(sb)
        bias_row = pl.multiple_of((A_LEFT_BLOCKS - (i * A_STEP_BLOCKS + sb - blk0)) * A_Q, A_Q)
        s = s_ref[t % 2] + bias_ref[p, pl.ds(bias_row, A_WIN), :]
        m = jnp.max(s, axis=0, keepdims=True)
        p_ref[t % 2] = jnp.exp2(s - m).astype(jnp.bfloat16)

    def finish(t):
        sb, p = items[t]
        cols = slice(p * LANES, (p + 1) * LANES)
        rows = slice(sb * A_Q, (sb + 1) * A_Q)
        blk0 = first_key_block(sb)
        vw = jnp.concatenate([vt_ref[0, blk0 + u, cols, :] for u in range(A_WIN // LANES)], axis=1)
        vw = jnp.concatenate([vw, ones], axis=0)
        r = jnp.dot(vw, p_ref[t % 2], preferred_element_type=jnp.float32)
        r = r[:LANES] * (1.0 / r[LANES:LANES + 1])
        ot = jnp.concatenate([r[:HEAD_DIM, :A_Q], r[HEAD_DIM:, A_Q:]], axis=0)
        g = g_ref[0, rows, cols].astype(jnp.float32)
        o_ref[0, rows, cols] = (ot.T * _silu(g)).astype(o_ref.dtype)

    n_items = len(items)
    scores(0)
    scores(1)
    softmax(0)
    for t in range(n_items):
        if t + 2 < n_items:
            scores(t + 2)
        if t + 1 < n_items:
            softmax(t + 1)
        finish(t)


def _attn_a(q_all, k_all, vt_all, g_all, bias_rows, width_a, layer):
    B, S, _ = q_all.shape
    step = A_STEP_BLOCKS * A_Q
    grid = (B, S // step)
    q_block = pl.BlockSpec((1, step, width_a), lambda b, i: (b, i, 0))
    return pl.pallas_call(
        _attn_a_kernel,
        grid=grid,
        in_specs=[
            q_block,
            pl.BlockSpec((1, S, width_a), lambda b, i: (b, 0, 0)),
            pl.BlockSpec((1, S // LANES, width_a, LANES), lambda b, i: (b, 0, 0, 0)),
            q_block,
            pl.BlockSpec((1,) + bias_rows.shape[1:], lambda b, i: (layer, 0, 0, 0)),
        ],
        out_specs=q_block,
        out_shape=jax.ShapeDtypeStruct((B, S, width_a), jnp.bfloat16),
        scratch_shapes=[
            pltpu.VMEM((width_a // LANES, A_BIAS_ROWS, 2 * A_Q), jnp.float32),
            pltpu.VMEM((2, A_WIN, 2 * A_Q), jnp.float32),
            pltpu.VMEM((2, A_WIN, 2 * A_Q), jnp.bfloat16),
        ],
        compiler_params=pltpu.CompilerParams(
            dimension_semantics=("arbitrary", "arbitrary"), vmem_limit_bytes=VMEM_LIMIT_BYTES),
    )(q_all, k_all, vt_all, g_all, bias_rows)


def _bias_rows(rel_bias):
    n_blk = A_BIAS_ROWS // A_Q
    reach = A_BIAS_ROWS
    table = jnp.pad(rel_bias.astype(jnp.float32) * LOG2E, ((0, 0), (0, 0), (reach, reach)), mode="edge")
    centre = reach + REL_CLIP
    rows = []
    for a in range(n_blk):
        rel0 = centre + (A_LEFT_BLOCKS - a) * A_Q
        rows.append(jnp.concatenate([table[..., rel0:rel0 + A_Q], table[..., rel0 - A_Q:rel0]], axis=-1))
    return jnp.stack(rows, axis=2)


def _fill_bias_table(f_ref, bias_ref):
    _, n_heads, n_blk, _ = f_ref.shape
    row = lax.broadcasted_iota(jnp.int32, (A_Q, A_Q), 0)
    qchunk = lax.broadcasted_iota(jnp.int32, (A_Q, A_Q), 1) // CHUNK
    for h in range(n_heads):
        pair, half = divmod(h, 2)
        for a in range(n_blk):
            f = jnp.broadcast_to(f_ref[0, h, a:a + 1, :], (A_Q, 2 * A_Q))
            t = pltpu.roll(f, 0, 1, stride=1, stride_axis=0)[:, :A_Q]
            dchunk = qchunk - (row + a * A_Q) // CHUNK + N_LEFT_CHUNKS
            valid = (dchunk >= 0) & (dchunk <= N_LEFT_CHUNKS)
            bias_ref[pair, a * A_Q:(a + 1) * A_Q, half * A_Q:(half + 1) * A_Q] = jnp.where(valid, t, NEG_INF)


def _attn_b_kernel(q_ref, k_ref, vt_ref, g_ref, lq1_ref, lk1_ref, lq2_ref, lk2_ref, sub_ref,
                   o_ref, qm_ref, s_ref, bm_ref, m_ref, acc_ref, *, lam_init):
    n_q = q_ref.shape[1] // B_Q
    heads = range(B_HEADS)
    lane = lax.broadcasted_iota(jnp.int32, (B_Q, LANES), 1)
    first = lane < HEAD_DIM
    ones = jnp.ones((BF16_SUBLANES, B_K), jnp.bfloat16)
    blocks_per_step = B_K // LANES
    lam = (jnp.exp(jnp.sum(lq1_ref[0] * lk1_ref[0], axis=1, keepdims=True))
           - jnp.exp(jnp.sum(lq2_ref[0] * lk2_ref[0], axis=1, keepdims=True)) + lam_init)

    def head_lanes(t):
        return slice(t * LANES, (t + 1) * LANES)

    def scores(j, slot, which=heads, maps=(0, 1)):
        for t in which:
            kj = k_ref[0, pl.ds(pl.multiple_of(j * B_K, B_K), B_K), head_lanes(t)]
            for n in maps:
                s = lax.dot_general(kj, qm_ref[t, n], _NT, preferred_element_type=jnp.float32)
                s_ref[t, slot, n] = s
                bm_ref[t, slot, n] = jnp.max(s, axis=0, keepdims=True)

    def accumulate(j, slot, masked=False, which=heads, maps=(0, 1)):
        for t in which:
            vj = jnp.concatenate([vt_ref[0, j * blocks_per_step + u, head_lanes(t), :]
                                  for u in range(blocks_per_step)], axis=1)
            vj = jnp.concatenate([vj, ones], axis=0)
            for n in maps:
                s = s_ref[t, slot, n]
                if masked:
                    kc = lax.broadcasted_iota(jnp.int32, (B_K, B_Q), 0) // CHUNK
                    qc = lax.broadcasted_iota(jnp.int32, (B_K, B_Q), 1) // CHUNK
                    s = jnp.where(kc <= qc, s, NEG_INF)
                    block_max = jnp.max(s, axis=0, keepdims=True)
                else:
                    block_max = bm_ref[t, slot, n]
                m_old = m_ref[t, n]
                m_new = jnp.maximum(m_old, block_max)
                alpha = jnp.exp2(m_old - m_new)
                p = jnp.exp2(s - m_new).astype(jnp.bfloat16)
                acc_ref[t, n] = alpha * acc_ref[t, n] + jnp.dot(vj, p, preferred_element_type=jnp.float32)
                m_ref[t, n] = m_new

    def two_blocks(jj, carry):
        j = 2 * jj
        for t in heads:
            for n in range(2):
                scores(j + 1, 1, which=(t,), maps=(n,))
                accumulate(j, 0, which=(t,), maps=(n,))
        for t in heads:
            for n in range(2):
                scores(j + 2, 0, which=(t,), maps=(n,))
                accumulate(j + 1, 1, which=(t,), maps=(n,))
        return carry

    def load_queries(i):
        rows = pl.ds(pl.multiple_of(i * B_Q, B_Q), B_Q)
        for t in heads:
            qp = q_ref[0, rows, head_lanes(t)]
            zero = jnp.zeros_like(qp)
            qm_ref[t, 0] = jnp.where(first, qp, zero)
            qm_ref[t, 1] = jnp.where(first, zero, qp)

    def query_block(i, carry):
        rows = pl.ds(pl.multiple_of(i * B_Q, B_Q), B_Q)
        acc_ref[...] = jnp.zeros_like(acc_ref)
        m_ref[...] = jnp.full(m_ref.shape, NEG_INF, jnp.float32)
        lax.fori_loop(0, i // 2, two_blocks, 0)

        @pl.when(i % 2 == 1)
        def _():
            for t in heads:
                for n in range(2):
                    scores(i, 1, which=(t,), maps=(n,))
                    accumulate(i - 1, 0, which=(t,), maps=(n,))

        accumulate(i, i % 2, masked=True)
        load_queries(jnp.minimum(i + 1, n_q - 1))
        scores(0, 0)

        for t in heads:
            l0 = acc_ref[t, 0, LANES:LANES + 1, :]
            l1 = acc_ref[t, 1, LANES:LANES + 1, :]
            ot = (acc_ref[t, 0, :LANES, :] * (1.0 / l0)
                  - lam * (acc_ref[t, 1, :LANES, :] * (1.0 / l1)))
            ot = ot * lax.rsqrt(jnp.mean(ot * ot, axis=0, keepdims=True) + RMS_EPS)
            o = ot.T * (sub_ref[0] * (1.0 - lam_init))
            g = g_ref[0, rows, head_lanes(t)].astype(jnp.float32)
            o_ref[0, rows, head_lanes(t)] = (o * _silu(g)).astype(o_ref.dtype)
        return carry

    load_queries(0)
    scores(0, 0)
    lax.fori_loop(0, n_q, query_block, 0)


def _attn_b(q_all, k_all, vt_all, g_all, lq1, lk1, lq2, lk2, subln, width_a, lam_init, layer):
    B, S, width = q_all.shape
    step_lanes = B_HEADS * LANES
    col0 = width_a // step_lanes
    grid = (B, (width - width_a) // step_lanes)
    seq_block = pl.BlockSpec((1, S, step_lanes), lambda b, h: (b, 0, col0 + h))
    vec64 = pl.BlockSpec((1, 1, HEAD_DIM), lambda b, h: (layer, 0, 0))
    return pl.pallas_call(
        functools.partial(_attn_b_kernel, lam_init=lam_init),
        grid=grid,
        in_specs=[
            seq_block,
            seq_block,
            pl.BlockSpec((1, S // LANES, step_lanes, LANES), lambda b, h: (b, 0, col0 + h, 0)),
            seq_block,
            vec64, vec64, vec64, vec64,
            pl.BlockSpec((1, 1, LANES), lambda b, h: (layer, 0, 0)),
        ],
        out_specs=pl.BlockSpec((1, S, step_lanes), lambda b, h: (b, 0, h)),
        out_shape=jax.ShapeDtypeStruct((B, S, width - width_a), jnp.bfloat16),
        scratch_shapes=[
            pltpu.VMEM((B_HEADS, 2, B_Q, LANES), jnp.bfloat16),
            pltpu.VMEM((B_HEADS, 2, 2, B_K, B_Q), jnp.float32),
            pltpu.VMEM((B_HEADS, 2, 2, 1, B_Q), jnp.float32),
            pltpu.VMEM((B_HEADS, 2, 1, B_Q), jnp.float32),
            pltpu.VMEM((B_HEADS, 2, ACC_ROWS, B_Q), jnp.float32),
        ],
        compiler_params=pltpu.CompilerParams(
            dimension_semantics=("arbitrary", "arbitrary"), vmem_limit_bytes=VMEM_LIMIT_BYTES),
    )(q_all, k_all, vt_all, g_all, lq1, lk1, lq2, lk2, subln)


def _out_proj_kernel(x_ref, ya_ref, yb_ref, w_ref, g_ref, b_ref, o_ref, wb_ref, *, alpha):
    width_a = ya_ref.shape[2]

    @pl.when((pl.program_id(0) == 0) & (pl.program_id(1) == 0))
    def _():
        wb_ref[...] = w_ref[0].astype(jnp.bfloat16)

    for r in range(x_ref.shape[1] // OUT_SUB_ROWS):
        rows = slice(r * OUT_SUB_ROWS, (r + 1) * OUT_SUB_ROWS)
        y = jnp.dot(ya_ref[0, rows, :], wb_ref[:width_a, :], preferred_element_type=jnp.float32)
        y = y + jnp.dot(yb_ref[0, rows, :], wb_ref[width_a:, :], preferred_element_type=jnp.float32)
        z = alpha * x_ref[0, rows, :] + y
        mu = jnp.mean(z, axis=1, keepdims=True)
        zc = z - mu
        var = jnp.mean(zc * zc, axis=1, keepdims=True)
        o_ref[0, rows, :] = zc * lax.rsqrt(var + LN_EPS) * g_ref[0] + b_ref[0]


def _out_proj(x, ya, yb, w, ln_g, ln_b, alpha, layer):
    B, S, D = x.shape
    grid = (B, S // OUT_ROWS)
    x_block = pl.BlockSpec((1, OUT_ROWS, D), lambda b, i: (b, i, 0))
    vec = pl.BlockSpec((1, 1, D), lambda b, i: (layer, 0, 0))
    return pl.pallas_call(
        functools.partial(_out_proj_kernel, alpha=alpha),
        grid=grid,
        in_specs=[
            x_block,
            pl.BlockSpec((1, OUT_ROWS, ya.shape[2]), lambda b, i: (b, i, 0)),
            pl.BlockSpec((1, OUT_ROWS, yb.shape[2]), lambda b, i: (b, i, 0)),
            pl.BlockSpec((1,) + w.shape[1:], lambda b, i: (layer, 0, 0), pipeline_mode=pl.Buffered(1)),
            vec, vec,
        ],
        out_specs=x_block,
        out_shape=jax.ShapeDtypeStruct((B, S, D), jnp.float32),
        scratch_shapes=[pltpu.VMEM(w.shape[1:], jnp.bfloat16)],
        compiler_params=pltpu.CompilerParams(
            dimension_semantics=("arbitrary", "arbitrary"), vmem_limit_bytes=VMEM_LIMIT_BYTES),
    )(x, ya, yb, w, ln_g, ln_b)


def _rope_tables(seq):
    half = HEAD_DIM // 2
    inv_freq = ROPE_THETA ** (-jnp.arange(0, HEAD_DIM, 2, dtype=jnp.float32) / HEAD_DIM)
    pos = jnp.arange(seq, dtype=jnp.float32)
    ang = pos[:, None] * inv_freq[None, :]
    reps = LANES // half
    cos = jnp.concatenate([jnp.cos(ang)] * reps, axis=-1)
    sin = jnp.concatenate([jnp.sin(ang)] * reps, axis=-1)
    low = (np.arange(LANES) % HEAD_DIM) < half
    return cos, jnp.where(low, -sin, 0.0), jnp.where(low, 0.0, sin)


def kernel(x, w_in, w_out, rel_bias, lambda_q1, lambda_k1, lambda_q2, lambda_k2, subln_g, ln_g, ln_b):
    depth, d_model, proj_cols = w_in.shape
    seq = x.shape[1]
    width_a = proj_cols // 8
    alpha = (2 * depth) ** 0.25
    scale = HEAD_DIM ** -0.5
    cos, sina, sinb = _rope_tables(seq)
    q_scale = scale * LOG2E
    bias_rows = _bias_rows(rel_bias)
    lq1, lk1, lq2, lk2, subln, g_ln, b_ln = (
        p.astype(jnp.float32)[:, None, :] for p in (lambda_q1, lambda_k1, lambda_q2, lambda_k2, subln_g, ln_g, ln_b))
    for l in range(depth):
        q_all, k_all, g_all, vt_all = _in_proj(x, w_in, cos, sina, sinb, width_a, q_scale, l)
        ya = _attn_a(q_all, k_all, vt_all, g_all, bias_rows, width_a, l)
        lam_init = 0.8 - 0.6 * math.exp(-0.3 * l)
        yb = _attn_b(q_all, k_all, vt_all, g_all, lq1, lk1, lq2, lk2, subln, width_a, lam_init, l)
        x = _out_proj(x, ya, yb, w_out, g_ln, b_ln, alpha, l)
    return x
```
